```python
import math
import jax, jax.numpy as jnp
from jax import lax
import numpy as np

D_MODEL = 1024
BATCH = 16
SEQ = 2048
DEPTH = 2

N_META = 16
NORM_EPS = 1e-6
N_MIXERS = 2
N_MLA = (DEPTH + 1) // 2
N_LRU = DEPTH // 2

MLA_HEADS = 8
QK_NOPE = 128
QK_ROPE = 64
V_HEAD = 128
Q_LORA = 384
KV_LORA = 256
ROPE_THETA = 10000.0
Q_BLOCK = 128

LRU_WIDTH = D_MODEL
LRU_BLOCKS = 8
LRU_BW = LRU_WIDTH // LRU_BLOCKS
CONV_W = 4
LRU_C = 8.0

PEER_HEADS = 8
N_KEYS = 128
N_EXPERTS = N_KEYS * N_KEYS
PEER_TOPK = 16
PEER_QDIM = 256
PEER_QHALF = PEER_QDIM // 2
PEER_BLOCK = 256

kernel_name = "hybrid_mla_rglru_peer_meta"


def rms_norm(x, g):
    x32 = x.astype(jnp.float32)
    y = x32 * lax.rsqrt(jnp.mean(x32 * x32, axis=-1, keepdims=True) + NORM_EPS)
    return (y * g.astype(jnp.float32)).astype(x.dtype)


def rope(x, pos):
    half = QK_ROPE // 2
    freqs = ROPE_THETA ** (-jnp.arange(half, dtype=jnp.float32) / half)
    ang = pos.astype(jnp.float32)[..., None] * freqs
    cos = jnp.cos(ang)[:, :, None, :]
    sin = jnp.sin(ang)[:, :, None, :]
    x32 = x.astype(jnp.float32)
    x1, x2 = x32[..., :half], x32[..., half:]
    return jnp.concatenate([x1 * cos - x2 * sin, x2 * cos + x1 * sin], axis=-1).astype(x.dtype)


def mla_mixer(h, pos, w_in, q_norm, w_q_up, kv_norm, w_kv_up, w_out):
    B, L, _ = h.shape
    a = h @ w_in
    q_lat = a[..., :Q_LORA]
    kv_lat = a[..., Q_LORA:Q_LORA + KV_LORA]
    k_rot = a[..., Q_LORA + KV_LORA:]
    q = (rms_norm(q_lat, q_norm) @ w_q_up).reshape(B, L, MLA_HEADS, QK_NOPE + QK_ROPE)
    kv = (rms_norm(kv_lat, kv_norm) @ w_kv_up).reshape(B, L, MLA_HEADS, QK_NOPE + V_HEAD)
    q = jnp.concatenate([q[..., :QK_NOPE], rope(q[..., QK_NOPE:], pos)], axis=-1)
    k_pe = rope(k_rot[:, :, None, :], pos)
    k = jnp.concatenate([kv[..., :QK_NOPE],
                         jnp.broadcast_to(k_pe, (B, L, MLA_HEADS, QK_ROPE))], axis=-1)
    v = kv[..., QK_NOPE:]
    n_qb = -(-L // Q_BLOCK)
    Lp = n_qb * Q_BLOCK
    pad = ((0, 0), (0, Lp - L), (0, 0), (0, 0))
    q, k, v = jnp.pad(q, pad), jnp.pad(k, pad), jnp.pad(v, pad)
    k32 = k.astype(jnp.float32)
    scale = 1.0 / math.sqrt(QK_NOPE + QK_ROPE)
    key_idx = jnp.arange(Lp)
    q_blocks = q.reshape(B, n_qb, Q_BLOCK, MLA_HEADS, QK_NOPE + QK_ROPE).transpose(1, 0, 2, 3, 4)

    def attend(args):
        qi, qb = args
        s = jnp.einsum('bqhd,bkhd->bhqk', qb.astype(jnp.float32), k32) * scale
        q_idx = qi * Q_BLOCK + jnp.arange(Q_BLOCK)
        causal = key_idx[None, :] <= q_idx[:, None]
        s = jnp.where(causal[None, None], s, -1e30)
        p = jax.nn.softmax(s, axis=-1)
        return jnp.einsum('bhqk,bkhd->bqhd', p.astype(v.dtype), v)

    o = lax.map(attend, (jnp.arange(n_qb), q_blocks))
    o = o.transpose(1, 0, 2, 3, 4).reshape(B, Lp, MLA_HEADS * V_HEAD)[:, :L]
    return o @ w_out


def rglru_mixer(h, w_in, conv_w, conv_b, w_gate_a, b_gate_a, w_gate_x, b_gate_x, lam, w_out):
    B, L, _ = h.shape
    proj = h @ w_in
    gate = jax.nn.gelu(proj[..., :LRU_WIDTH], approximate=True)
    xr = proj[..., LRU_WIDTH:]
    xpad = jnp.pad(xr, ((0, 0), (CONV_W - 1, 0), (0, 0)))
    xc = conv_b
    for j in range(CONV_W):
        xc = xc + xpad[:, j:j + L] * conv_w[j]
    xb = xc.reshape(B, L, LRU_BLOCKS, LRU_BW)
    r = jax.nn.sigmoid((jnp.einsum('blnd,nde->blne', xb, w_gate_a).reshape(B, L, LRU_WIDTH)
                        + b_gate_a).astype(jnp.float32))
    i = jax.nn.sigmoid((jnp.einsum('blnd,nde->blne', xb, w_gate_x).reshape(B, L, LRU_WIDTH)
                        + b_gate_x).astype(jnp.float32))
    log_a = -LRU_C * r * jax.nn.softplus(-lam.astype(jnp.float32))
    a = jnp.exp(log_a)
    mult = jnp.sqrt(-jnp.expm1(2.0 * log_a))
    b = mult * (i * xc.astype(jnp.float32))

    def combine(e1, e2):
        a1, b1 = e1
        a2, b2 = e2
        return a1 * a2, a2 * b1 + b2

    _, hs = lax.associative_scan(combine, (a, b), axis=1)
    y = hs.astype(h.dtype) * gate
    return y @ w_out


def peer_ffn(h, w_query, sub_keys, u_tab, v_tab):
    B, L, D = h.shape
    tok = h.reshape(B * L, D)
    T = B * L
    n_blk = -(-T // PEER_BLOCK)
    tok = jnp.pad(tok, ((0, n_blk * PEER_BLOCK - T), (0, 0))).reshape(n_blk, PEER_BLOCK, D)

    def retrieve(xb):
        q = (xb @ w_query).reshape(-1, PEER_HEADS, 2, PEER_QHALF)
        s = jnp.einsum('thpd,pkd->thpk', q, sub_keys).astype(jnp.float32)
        sv, si = lax.top_k(s, PEER_TOPK)
        cand = (sv[:, :, 0, :, None] + sv[:, :, 1, None, :]).reshape(-1, PEER_HEADS, PEER_TOPK * PEER_TOPK)
        cidx = (si[:, :, 0, :, None] * N_KEYS + si[:, :, 1, None, :]).reshape(-1, PEER_HEADS, PEER_TOPK * PEER_TOPK)
        top_s, top_p = lax.top_k(cand, PEER_TOPK)
        eidx = jnp.take_along_axis(cidx, top_p, axis=-1)
        g = jax.nn.softmax(top_s, axis=-1)
        u = u_tab[eidx]
        act = jax.nn.gelu(jnp.einsum('thkd,td->thk', u, xb).astype(jnp.float32))
        w = (g * act).astype(xb.dtype)
        return jnp.einsum('thk,thkd->td', w, v_tab[eidx])

    out = lax.map(retrieve, tok)
    return out.reshape(n_blk * PEER_BLOCK, D)[:T].reshape(B, L, D)


def setup_inputs(seed: int = 0) -> dict:
    key = jax.random.key(seed)
    ks = iter(jax.random.split(key, 32))
    f32 = jnp.float32

    def nrm(shape, scale):
        return jax.random.normal(next(ks), shape, f32) * scale

    def gain(shape):
        return 1.0 + 0.05 * jax.random.normal(next(ks), shape, f32)

    x = jax.random.normal(next(ks), (BATCH, SEQ, D_MODEL), f32)
    positions = jnp.broadcast_to(jnp.arange(SEQ, dtype=jnp.int32)[None, :], (BATCH, SEQ))
    meta_tokens = nrm((N_META, D_MODEL), 1.0)
    mixer_norm = gain((DEPTH, D_MODEL))
    ffn_norm = gain((DEPTH, D_MODEL))
    final_norm = gain((D_MODEL,))

    mla_w_in = nrm((N_MLA, D_MODEL, Q_LORA + KV_LORA + QK_ROPE), D_MODEL ** -0.5)
    mla_q_norm = gain((N_MLA, Q_LORA))
    mla_w_q_up = nrm((N_MLA, Q_LORA, MLA_HEADS * (QK_NOPE + QK_ROPE)), Q_LORA ** -0.5)
    mla_kv_norm = gain((N_MLA, KV_LORA))
    mla_w_kv_up = nrm((N_MLA, KV_LORA, MLA_HEADS * (QK_NOPE + V_HEAD)), KV_LORA ** -0.5)
    mla_w_out = nrm((N_MLA, MLA_HEADS * V_HEAD, D_MODEL), (MLA_HEADS * V_HEAD) ** -0.5)

    lru_w_in = nrm((N_LRU, D_MODEL, 2 * LRU_WIDTH), D_MODEL ** -0.5)
    lru_conv_w = nrm((N_LRU, CONV_W, LRU_WIDTH), CONV_W ** -0.5)
    lru_conv_b = nrm((N_LRU, LRU_WIDTH), 0.01)
    lru_w_gate_a = nrm((N_LRU, LRU_BLOCKS, LRU_BW, LRU_BW), LRU_BW ** -0.5)
    lru_b_gate_a = nrm((N_LRU, LRU_WIDTH), 0.01)
    lru_w_gate_x = nrm((N_LRU, LRU_BLOCKS, LRU_BW, LRU_BW), LRU_BW ** -0.5)
    lru_b_gate_x = nrm((N_LRU, LRU_WIDTH), 0.01)
    u = jax.random.uniform(next(ks), (N_LRU, LRU_WIDTH), f32, 0.9, 0.999)
    s = u ** (1.0 / LRU_C)
    lru_lambda = jnp.log(s) - jnp.log1p(-s)
    lru_w_out = nrm((N_LRU, LRU_WIDTH, D_MODEL), LRU_WIDTH ** -0.5)

    peer_w_query = nrm((DEPTH, D_MODEL, PEER_HEADS * PEER_QDIM), D_MODEL ** -0.5)
    peer_sub_keys = nrm((DEPTH, 2, N_KEYS, PEER_QHALF), PEER_QHALF ** -0.5)
    peer_u = nrm((DEPTH, N_EXPERTS, D_MODEL), D_MODEL ** -0.5)
    peer_v = nrm((DEPTH, N_EXPERTS, D_MODEL), PEER_TOPK ** -0.5)

    return {"x": x, "positions": positions, "meta_tokens": meta_tokens,
            "mixer_norm": mixer_norm, "ffn_norm": ffn_norm, "final_norm": final_norm,
            "mla_w_in": mla_w_in, "mla_q_norm": mla_q_norm, "mla_w_q_up": mla_w_q_up,
            "mla_kv_norm": mla_kv_norm, "mla_w_kv_up": mla_w_kv_up, "mla_w_out": mla_w_out,
            "lru_w_in": lru_w_in, "lru_conv_w": lru_conv_w, "lru_conv_b": lru_conv_b,
            "lru_w_gate_a": lru_w_gate_a, "lru_b_gate_a": lru_b_gate_a,
            "lru_w_gate_x": lru_w_gate_x, "lru_b_gate_x": lru_b_gate_x,
            "lru_lambda": lru_lambda, "lru_w_out": lru_w_out,
            "peer_w_query": peer_w_query, "peer_sub_keys": peer_sub_keys,
            "peer_u": peer_u, "peer_v": peer_v}


def reference(x, positions, meta_tokens, mixer_norm, ffn_norm, final_norm,
              mla_w_in, mla_q_norm, mla_w_q_up, mla_kv_norm, mla_w_kv_up, mla_w_out,
              lru_w_in, lru_conv_w, lru_conv_b, lru_w_gate_a, lru_b_gate_a,
              lru_w_gate_x, lru_b_gate_x, lru_lambda, lru_w_out,
              peer_w_query, peer_sub_keys, peer_u, peer_v):
    B = x.shape[0]
    meta = jnp.broadcast_to(meta_tokens[None].astype(x.dtype), (B, N_META, x.shape[-1]))
    h = jnp.concatenate([meta, x], axis=1)
    meta_pos = jnp.broadcast_to(jnp.arange(N_META, dtype=jnp.int32)[None], (B, N_META))
    pos = jnp.concatenate([meta_pos, positions.astype(jnp.int32) + N_META], axis=1)
    for layer in range(DEPTH):
        hn = rms_norm(h, mixer_norm[layer])
        j = layer // N_MIXERS
        if layer % N_MIXERS == 0:
            mix = mla_mixer(hn, pos, mla_w_in[j], mla_q_norm[j], mla_w_q_up[j],
                            mla_kv_norm[j], mla_w_kv_up[j], mla_w_out[j])
        else:
            mix = rglru_mixer(hn, lru_w_in[j], lru_conv_w[j], lru_conv_b[j],
                              lru_w_gate_a[j], lru_b_gate_a[j], lru_w_gate_x[j],
                              lru_b_gate_x[j], lru_lambda[j], lru_w_out[j])
        h = h + mix
        h = h + peer_ffn(rms_norm(h, ffn_norm[layer]), peer_w_query[layer],
                         peer_sub_keys[layer], peer_u[layer], peer_v[layer])
    out = rms_norm(h, final_norm)
    return out[:, N_META:]
```

```python
import functools
import math

import jax
import jax.numpy as jnp
from jax import lax
from jax.experimental import pallas as pl
from jax.experimental.pallas import tpu as pltpu

F32 = jnp.float32
BF16 = jnp.bfloat16

N_META = 16
NORM_EPS = 1e-6
MLA_HEADS = 8
QK_NOPE = 128
QK_ROPE = 64
V_HEAD = 128
Q_LORA = 384
KV_LORA = 256
ROPE_THETA = 10000.0
HEAD_PAD = 256
LRU_BLOCKS = 8
LRU_BW = 128
CONV_W = 4
LRU_C = 8.0
PEER_HEADS = 8
N_KEYS = 128
PEER_TOPK = 16
GELU_K0 = math.sqrt(2.0 / math.pi)
GELU_K1 = GELU_K0 * 0.044715
NEG_INF = float("-inf")

ROW_BLOCK = 256
ATTN_BLOCK = 512
PEER_TOKENS = 768
PEER_EXPERTS = 1024
PEER_SUB = 256
VMEM_LIMIT = 58 * 1024 * 1024


def _dot(a, b):
    return jnp.dot(a, b, preferred_element_type=F32)


def _dot_nt(a, b):
    return lax.dot_general(a, b, (((1,), (1,)), ((), ())), preferred_element_type=F32)


def _rms(x, g):
    return x * lax.rsqrt(jnp.mean(x * x, axis=-1, keepdims=True) + NORM_EPS) * g


def _gelu(x):
    return 0.5 * x * (1.0 + jnp.tanh(x * (GELU_K0 + GELU_K1 * x * x)))


def _params(*sem):
    return pltpu.CompilerParams(dimension_semantics=sem, vmem_limit_bytes=VMEM_LIMIT)


def _full(shape):
    return pl.BlockSpec(shape, lambda *_: (0,) * len(shape))


def _mla_qkv_kernel(h_ref, g_ref, win_ref, qn_ref, kvn_ref, wq_ref, wkv_ref, c_ref, s_ref,
                    q_ref, k_ref, v_ref):
    hn = _rms(h_ref[...], g_ref[...]).astype(BF16)
    a = _dot(hn, win_ref[...])
    q_lat = a[:, :Q_LORA]
    kv_lat = a[:, Q_LORA:Q_LORA + KV_LORA]
    k_pe = a[:, Q_LORA + KV_LORA:Q_LORA + KV_LORA + HEAD_PAD]
    k_pe_rot = a[:, Q_LORA + KV_LORA + HEAD_PAD:]
    cos = c_ref[...]
    sin = s_ref[...]
    k_pe = k_pe * cos + k_pe_rot * sin
    qn = _rms(q_lat, qn_ref[...]).astype(BF16)
    kvn = _rms(kv_lat, kvn_ref[...]).astype(BF16)
    qq = _dot(qn, wq_ref[...])
    kv = _dot(kvn, wkv_ref[...])
    scale = 1.0 / math.sqrt(QK_NOPE + QK_ROPE)
    width = MLA_HEADS * HEAD_PAD
    for h in range(MLA_HEADS):
        lo, hi = h * HEAD_PAD, (h + 1) * HEAD_PAD
        q_h = (qq[:, lo:hi] * cos + qq[:, width + lo:width + hi] * sin) * scale
        q_ref[:, lo:hi] = q_h.astype(BF16)
        k_ref[:, lo:hi] = (kv[:, lo:hi] + k_pe).astype(BF16)
    v_ref[...] = kv[:, width:].astype(BF16)


def _mla_qkv(h, g, win, qn, kvn, wq, wkv, cos, sin):
    t = h.shape[0]
    d = h.shape[1]
    width = MLA_HEADS * HEAD_PAD
    rows = lambda w: pl.BlockSpec((ROW_BLOCK, w), lambda i: (i, 0))
    return pl.pallas_call(
        _mla_qkv_kernel,
        grid=(t // ROW_BLOCK,),
        in_specs=[rows(d), _full(g.shape), _full(win.shape), _full(qn.shape), _full(kvn.shape),
                  _full(wq.shape), _full(wkv.shape), rows(HEAD_PAD), rows(HEAD_PAD)],
        out_specs=[rows(width), rows(width), rows(MLA_HEADS * V_HEAD)],
        out_shape=[jax.ShapeDtypeStruct((t, width), BF16), jax.ShapeDtypeStruct((t, width), BF16),
                   jax.ShapeDtypeStruct((t, MLA_HEADS * V_HEAD), BF16)],
        compiler_params=_params("parallel"),
        name="mla_qkv",
    )(h, g, win, qn, kvn, wq, wkv, cos, sin)


def _attn_kernel(q_ref, k_ref, v_ref, qm_ref, km_ref, vm_ref, o_ref, om_ref, *, blk):
    qi = pl.program_id(1)
    q = q_ref[...]
    km = km_ref[...]
    vm = vm_ref[...]

    s = _dot_nt(q, km)
    m = jnp.max(s, axis=1, keepdims=True)
    p = jnp.exp(s - m)
    l = jnp.sum(p, axis=1, keepdims=True)
    acc = _dot(p.astype(BF16), vm)

    def step(j, carry, diagonal):
        m, l, acc = carry
        off = pl.multiple_of(j * blk, blk)
        s = _dot_nt(q, k_ref[pl.ds(off, blk), :])
        if diagonal:
            row = lax.broadcasted_iota(jnp.int32, (blk, blk), 0)
            col = lax.broadcasted_iota(jnp.int32, (blk, blk), 1)
            s = jnp.where(col <= row, s, -1e30)
        m_new = jnp.maximum(m, jnp.max(s, axis=1, keepdims=True))
        alpha = jnp.exp(m - m_new)
        p = jnp.exp(s - m_new)
        l = alpha * l + jnp.sum(p, axis=1, keepdims=True)
        acc = alpha * acc + _dot(p.astype(BF16), v_ref[pl.ds(off, blk), :])
        return m_new, l, acc

    carry = lax.fori_loop(0, qi, lambda j, c: step(j, c, False), (m, l, acc))
    m, l, acc = step(qi, carry, True)
    o_ref[...] = (acc / l).astype(BF16)

    @pl.when(qi == 0)
    def _meta_queries():
        s = _dot_nt(qm_ref[...], km)
        row = lax.broadcasted_iota(jnp.int32, (N_META, N_META), 0)
        col = lax.broadcasted_iota(jnp.int32, (N_META, N_META), 1)
        s = jnp.where(col <= row, s, -1e30)
        p = jnp.exp(s - jnp.max(s, axis=1, keepdims=True))
        o = _dot(p.astype(BF16), vm) / jnp.sum(p, axis=1, keepdims=True)
        om_ref[...] = o.astype(BF16)


def _attention(q, k, v, nb, seq):
    width = MLA_HEADS * HEAD_PAD
    vwidth = MLA_HEADS * V_HEAD
    steps = seq + N_META
    q2 = q.reshape(steps, nb * width)
    k2 = k.reshape(steps, nb * width)
    v2 = v.reshape(steps, nb * vwidth)
    blk = min(ATTN_BLOCK, seq)
    meta_blk = seq // N_META
    o, om = pl.pallas_call(
        functools.partial(_attn_kernel, blk=blk),
        grid=(nb * MLA_HEADS, seq // blk),
        in_specs=[pl.BlockSpec((blk, HEAD_PAD), lambda bh, qi: (qi, bh)),
                  pl.BlockSpec((seq, HEAD_PAD), lambda bh, qi: (0, bh)),
                  pl.BlockSpec((seq, V_HEAD), lambda bh, qi: (0, bh)),
                  pl.BlockSpec((N_META, HEAD_PAD), lambda bh, qi: (meta_blk, bh)),
                  pl.BlockSpec((N_META, HEAD_PAD), lambda bh, qi: (meta_blk, bh)),
                  pl.BlockSpec((N_META, V_HEAD), lambda bh, qi: (meta_blk, bh))],
        out_specs=[pl.BlockSpec((blk, V_HEAD), lambda bh, qi: (qi, bh)),
                   pl.BlockSpec((N_META, V_HEAD), lambda bh, qi: (0, bh))],
        out_shape=[jax.ShapeDtypeStruct((seq, nb * vwidth), BF16),
                   jax.ShapeDtypeStruct((N_META, nb * vwidth), BF16)],
        compiler_params=_params("parallel", "arbitrary"),
        name="mla_attention",
    )(q2, k2, v2, q2, k2, v2)
    return jnp.concatenate([o.reshape(seq * nb, vwidth), om.reshape(N_META * nb, vwidth)], axis=0)


def _proj_res_kernel(o_ref, w_ref, h_ref, out_ref):
    out_ref[...] = h_ref[...] + _dot(o_ref[...], w_ref[...])


def _proj_res(o, w, h):
    t, d = h.shape
    rows = lambda width: pl.BlockSpec((ROW_BLOCK, width), lambda i: (i, 0))
    return pl.pallas_call(
        _proj_res_kernel,
        grid=(t // ROW_BLOCK,),
        in_specs=[rows(o.shape[1]), _full(w.shape), rows(d)],
        out_specs=rows(d),
        out_shape=jax.ShapeDtypeStruct((t, d), F32),
        compiler_params=_params("parallel"),
        name="mla_out_proj",
    )(o, w, h)


def _lru_kernel(h_ref, g_ref, win_ref, cw_ref, cb_ref, wa_ref, ba_ref, wx_ref, bx_ref, lam_ref,
                wout_ref, out_ref, xbuf, state, *, nb):
    rows, width = h_ref.shape
    hist = (CONV_W - 1) * nb

    @pl.when(pl.program_id(0) == 0)
    def _init():
        xbuf[0:hist, :] = jnp.zeros((hist, width), F32)
        state[...] = jnp.zeros_like(state)

    x = h_ref[...]
    hn = _rms(x, g_ref[...]).astype(BF16)
    proj = _dot(hn, win_ref[...])
    gate = _gelu(proj[:, :width])
    xr = proj[:, width:]
    xbuf[hist:hist + rows, :] = xr
    xc = cb_ref[...] + xr * cw_ref[CONV_W - 1:CONV_W, :]
    for j in range(CONV_W - 1):
        xc = xc + xbuf[j * nb:j * nb + rows, :] * cw_ref[j:j + 1, :]
    xbuf[0:hist, :] = xbuf[rows:rows + hist, :]

    xcb = xc.astype(BF16)
    ra, ix = [], []
    for n in range(LRU_BLOCKS):
        blk = xcb[:, n * LRU_BW:(n + 1) * LRU_BW]
        ra.append(_dot(blk, wa_ref[n]))
        ix.append(_dot(blk, wx_ref[n]))
    r = jax.nn.sigmoid(jnp.concatenate(ra, axis=1) + ba_ref[...])
    i = jax.nn.sigmoid(jnp.concatenate(ix, axis=1) + bx_ref[...])
    lam = lam_ref[...]
    softplus = jnp.maximum(-lam, 0.0) + jnp.log1p(jnp.exp(-jnp.abs(lam)))
    log_a = -LRU_C * r * softplus
    a = jnp.exp(log_a)
    th = jnp.tanh(log_a)
    b = jnp.sqrt(-2.0 * th / (1.0 - th)) * (i * xc)

    hcur = state[...]
    hs = []
    for s in range(rows // nb):
        hcur = a[s * nb:(s + 1) * nb] * hcur + b[s * nb:(s + 1) * nb]
        hs.append(hcur)
    state[...] = hcur
    y = (jnp.concatenate(hs, axis=0) * gate).astype(BF16)
    out_ref[...] = x + _dot(y, wout_ref[...])


def _lru_block(h, g, win, cw, cb, wa, ba, wx, bx, lam, wout, nb):
    t, d = h.shape
    nblk = t // ROW_BLOCK
    rows = pl.BlockSpec((ROW_BLOCK, d), lambda i: ((i + nblk - 1) % nblk, 0))
    args = (g, win, cw, cb, wa, ba, wx, bx, lam, wout)
    return pl.pallas_call(
        functools.partial(_lru_kernel, nb=nb),
        grid=(nblk,),
        in_specs=[rows] + [_full(a.shape) for a in args],
        out_specs=rows,
        out_shape=jax.ShapeDtypeStruct((t, d), F32),
        scratch_shapes=[pltpu.VMEM(((CONV_W - 1) * nb + ROW_BLOCK, d), F32), pltpu.VMEM((nb, d), F32)],
        compiler_params=_params("arbitrary"),
        name="rglru_block",
    )(h, *args)


def _top_values(work, count, emit):
    for rank in range(count):
        mx = jnp.max(work, axis=0, keepdims=True)
        emit(rank, mx)
        work = jnp.where(work == mx, NEG_INF, work)


def _peer_kernel(h_ref, g_ref, wqt_ref, sk_ref, u_ref, vt_ref, out_ref,
                 xn_s, s_s, vals_s, cand_s, e1_s, r0_s, e0_s, acc_s):
    c = pl.program_id(1)
    tb = h_ref.shape[0]
    nsub = u_ref.shape[0] // PEER_SUB
    ntop = PEER_TOPK + 1
    vrows = vals_s.shape[1]

    @pl.when(c == 0)
    def _prepare():
        xn_s[...] = _rms(h_ref[...], g_ref[...]).astype(BF16)

        def sub_scores(hp, carry):
            w = wqt_ref[pl.ds(pl.multiple_of(hp * N_KEYS, N_KEYS), N_KEYS), :]
            qt = _dot_nt(w, xn_s[...]).astype(BF16)
            s = _dot(sk_ref[hp % 2], qt)
            s_s[hp] = s

            def emit(rank, mx):
                vals_s[hp, rank:rank + 1, :] = mx
            _top_values(s, ntop, emit)
            vals_s[hp, ntop:vrows, :] = jnp.full((vrows - ntop, tb), NEG_INF, F32)
            return carry

        lax.fori_loop(0, 2 * PEER_HEADS, sub_scores, 0)

        def head_gates(h, carry):
            v0 = vals_s[2 * h]
            v1 = vals_s[2 * h + 1]
            cand_s[0:vrows, :] = v0[0:1] + v1
            for a in range(1, 8):
                cand_s[vrows + 8 * (a - 1):vrows + 8 * a, :] = v0[a:a + 1] + v1[0:8]
            cand_s[vrows + 56:vrows + 56 + 16, :] = v0[8:24] + v1[0:1]
            got = {}

            def emit(rank, mx):
                got[rank] = mx
            _top_values(cand_s[...], ntop, emit)
            z = sum(jnp.exp(got[k] - got[0]) for k in range(PEER_TOPK))
            thr = 0.5 * (got[PEER_TOPK - 1] + got[PEER_TOPK])
            s0 = s_s[2 * h]
            s1 = s_s[2 * h + 1]
            e1_s[h] = jnp.exp(s1 - v1[0:1])
            e0_s[h] = jnp.exp(s0 - v0[0:1]) * (0.5 / z)
            r0_s[h] = thr - s0
            return carry

        lax.fori_loop(0, PEER_HEADS, head_gates, 0)
        acc_s[...] = jnp.zeros_like(acc_s)

    def experts(sub, carry):
        off = pl.multiple_of(sub * PEER_SUB, PEER_SUB)
        st = _dot_nt(u_ref[pl.ds(off, PEER_SUB), :], xn_s[...])
        wt = []
        for ii in range(PEER_SUB // N_KEYS):
            i = c * (nsub * (PEER_SUB // N_KEYS)) + sub * (PEER_SUB // N_KEYS) + ii
            s = st[ii * N_KEYS:(ii + 1) * N_KEYS]
            gate = jnp.zeros_like(s)
            for h in range(PEER_HEADS):
                r = r0_s[h, pl.ds(i, 1), :]
                e = e0_s[h, pl.ds(i, 1), :]
                gate = gate + e * jnp.where(s_s[2 * h + 1] >= r, e1_s[h], 0.0)
            act = s + s * jnp.tanh(s * (GELU_K0 + GELU_K1 * s * s))
            wt.append((act * gate).astype(BF16))
        acc_s[...] += _dot(vt_ref[:, pl.ds(off, PEER_SUB)], jnp.concatenate(wt, axis=0))
        return carry

    lax.fori_loop(0, nsub, experts, 0)

    @pl.when(c == pl.num_programs(1) - 1)
    def _finish():
        out_ref[...] = h_ref[...] + acc_s[...].T


def _peer(h, g, wqt, sk, u, vt):
    t, d = h.shape
    tb = PEER_TOKENS
    n_exp = u.shape[0]
    vrows = 24
    return pl.pallas_call(
        _peer_kernel,
        grid=(t // tb, n_exp // PEER_EXPERTS),
        in_specs=[pl.BlockSpec((tb, d), lambda i, c: (i, 0)),
                  _full(g.shape), _full(wqt.shape), _full(sk.shape),
                  pl.BlockSpec((PEER_EXPERTS, d), lambda i, c: (c, 0)),
                  pl.BlockSpec((d, PEER_EXPERTS), lambda i, c: (0, c))],
        out_specs=pl.BlockSpec((tb, d), lambda i, c: (i, 0)),
        out_shape=jax.ShapeDtypeStruct((t, d), F32),
        scratch_shapes=[pltpu.VMEM((tb, d), BF16),
                        pltpu.VMEM((2 * PEER_HEADS, N_KEYS, tb), F32),
                        pltpu.VMEM((2 * PEER_HEADS, vrows, tb), F32),
                        pltpu.VMEM((vrows + 56 + 16, tb), F32),
                        pltpu.VMEM((PEER_HEADS, N_KEYS, tb), F32),
                        pltpu.VMEM((PEER_HEADS, N_KEYS, tb), F32),
                        pltpu.VMEM((PEER_HEADS, N_KEYS, tb), F32),
                        pltpu.VMEM((d, tb), F32)],
        compiler_params=_params("parallel", "arbitrary"),
        name="peer_ffn",
    )(h, g, wqt, sk, u, vt)


def _final_norm_kernel(h_ref, g_ref, out_ref):
    out_ref[...] = _rms(h_ref[...], g_ref[...])


def _final_norm(h, g, rows_out):
    d = h.shape[1]
    rows = pl.BlockSpec((ROW_BLOCK, d), lambda i: (i, 0))
    return pl.pallas_call(
        _final_norm_kernel,
        grid=(rows_out // ROW_BLOCK,),
        in_specs=[rows, _full(g.shape)],
        out_specs=rows,
        out_shape=jax.ShapeDtypeStruct((rows_out, d), F32),
        compiler_params=_params("parallel"),
        name="final_norm",
    )(h, g)


def _rot_cols(w):
    half = QK_ROPE // 2
    return jnp.concatenate([-w[:, half:], w[:, :half]], axis=1)


def _pad_rope(w):
    z = jnp.zeros((w.shape[0], QK_NOPE), w.dtype)
    z2 = jnp.zeros((w.shape[0], HEAD_PAD - QK_NOPE - QK_ROPE), w.dtype)
    return jnp.concatenate([z, w, z2], axis=1)


def _mla_weights(w_in, w_q_up, w_kv_up):
    w_kr = w_in[:, Q_LORA + KV_LORA:]
    win = jnp.concatenate([w_in[:, :Q_LORA + KV_LORA], _pad_rope(w_kr), _pad_rope(_rot_cols(w_kr))], axis=1)
    wq = w_q_up.reshape(Q_LORA, MLA_HEADS, QK_NOPE + QK_ROPE)
    pad = jnp.zeros((Q_LORA, MLA_HEADS, HEAD_PAD - QK_NOPE - QK_ROPE), w_q_up.dtype)
    nope0 = jnp.zeros((Q_LORA, MLA_HEADS, QK_NOPE), w_q_up.dtype)
    rope = wq[:, :, QK_NOPE:]
    half = QK_ROPE // 2
    rope_rot = jnp.concatenate([-rope[:, :, half:], rope[:, :, :half]], axis=2)
    q_plain = jnp.concatenate([wq, pad], axis=2).reshape(Q_LORA, MLA_HEADS * HEAD_PAD)
    q_rot = jnp.concatenate([nope0, rope_rot, pad], axis=2).reshape(Q_LORA, MLA_HEADS * HEAD_PAD)
    wkv = w_kv_up.reshape(KV_LORA, MLA_HEADS, QK_NOPE + V_HEAD)
    kpad = jnp.zeros((KV_LORA, MLA_HEADS, HEAD_PAD - QK_NOPE), w_kv_up.dtype)
    k_nope = jnp.concatenate([wkv[:, :, :QK_NOPE], kpad], axis=2).reshape(KV_LORA, MLA_HEADS * HEAD_PAD)
    v = wkv[:, :, QK_NOPE:].reshape(KV_LORA, MLA_HEADS * V_HEAD)
    return (win.astype(BF16), jnp.concatenate([q_plain, q_rot], axis=1).astype(BF16),
            jnp.concatenate([k_nope, v], axis=1).astype(BF16))


def _rope_tables(pos):
    half = QK_ROPE // 2
    freqs = ROPE_THETA ** (-jnp.arange(half, dtype=F32) / half)
    ang = pos.astype(F32)[:, None] * freqs
    n = pos.shape[0]
    ones = jnp.ones((n, QK_NOPE), F32)
    zeros = jnp.zeros((n, QK_NOPE), F32)
    tail = jnp.zeros((n, HEAD_PAD - QK_NOPE - QK_ROPE), F32)
    cos = jnp.concatenate([ones, jnp.cos(ang), jnp.cos(ang), tail], axis=1)
    sin = jnp.concatenate([zeros, jnp.sin(ang), jnp.sin(ang), tail], axis=1)
    return cos, sin


def kernel(x, positions, meta_tokens, mixer_norm, ffn_norm, final_norm, mla_w_in, mla_q_norm, mla_w_q_up, mla_kv_norm, mla_w_kv_up, mla_w_out, lru_w_in, lru_conv_w, lru_conv_b, lru_w_gate_a, lru_b_gate_a, lru_w_gate_x, lru_b_gate_x, lru_lambda, lru_w_out, peer_w_query, peer_sub_keys, peer_u, peer_v):
    nb, seq, d = x.shape
    assert nb % 8 == 0 and (nb * N_META) % ROW_BLOCK == 0 and (nb * seq) % ROW_BLOCK == 0
    assert (nb * (seq + N_META)) % PEER_TOKENS == 0 and seq % min(ATTN_BLOCK, seq) == 0
    row = lambda v: v.reshape(1, -1).astype(F32)

    h = jnp.concatenate([x.transpose(1, 0, 2).reshape(seq * nb, d),
                         jnp.repeat(meta_tokens.astype(x.dtype), nb, axis=0)], axis=0)
    pos = jnp.concatenate([positions.astype(jnp.int32).T.reshape(-1) + N_META,
                           jnp.repeat(jnp.arange(N_META, dtype=jnp.int32), nb)], axis=0)
    cos, sin = _rope_tables(pos)

    def peer(h, layer):
        wqt = peer_w_query[layer].T.astype(BF16)
        return _peer(h, row(ffn_norm[layer]), wqt, peer_sub_keys[layer].astype(BF16),
                     peer_u[layer].astype(BF16), peer_v[layer].T.astype(BF16))

    win, wq, wkv = _mla_weights(mla_w_in[0], mla_w_q_up[0], mla_w_kv_up[0])
    q, k, v = _mla_qkv(h, row(mixer_norm[0]), win, row(mla_q_norm[0]), row(mla_kv_norm[0]), wq, wkv, cos, sin)
    o = _attention(q, k, v, nb, seq)
    h = _proj_res(o, mla_w_out[0].astype(BF16), h)
    h = peer(h, 0)

    h = _lru_block(h, row(mixer_norm[1]), lru_w_in[0].astype(BF16), lru_conv_w[0].astype(F32),
                   row(lru_conv_b[0]), lru_w_gate_a[0].astype(BF16), row(lru_b_gate_a[0]),
                   lru_w_gate_x[0].astype(BF16), row(lru_b_gate_x[0]), row(lru_lambda[0]),
                   lru_w_out[0].astype(BF16), nb)
    h = peer(h, 1)

    out = _final_norm(h, row(final_norm), seq * nb)
    return out.reshape(seq, nb, d).transpose(1, 0, 2)
```

```python
import functools
import math

import jax
import jax.numpy as jnp
from jax import lax
from jax.experimental import pallas as pl
from jax.experimental.pallas import tpu as pltpu

F32 = jnp.float32
BF16 = jnp.bfloat16

N_META = 16
NORM_EPS = 1e-6
MLA_HEADS = 8
QK_NOPE = 128
QK_ROPE = 64
V_HEAD = 128
Q_LORA = 384
KV_LORA = 256
ROPE_THETA = 10000.0
HEAD_PAD = 256
LRU_BLOCKS = 8
LRU_BW = 128
CONV_W = 4
LRU_C = 8.0
PEER_HEADS = 8
N_KEYS = 128
PEER_TOPK = 16
GELU_K0 = math.sqrt(2.0 / math.pi)
GELU_K1 = GELU_K0 * 0.044715
NEG_INF = float("-inf")

ROW_BLOCK = 256
ATTN_BLOCK = 512
PEER_TOKENS = 768
PEER_EXPERTS = 2048
PEER_SUB = 256
VMEM_LIMIT = 58 * 1024 * 1024


def _dot(a, b):
    return jnp.dot(a, b, preferred_element_type=F32)


def _dot_nt(a, b):
    return lax.dot_general(a, b, (((1,), (1,)), ((), ())), preferred_element_type=F32)


def _rms(x, g):
    return x * lax.rsqrt(jnp.mean(x * x, axis=-1, keepdims=True) + NORM_EPS) * g


def _gelu(x):
    return 0.5 * x * (1.0 + jnp.tanh(x * (GELU_K0 + GELU_K1 * x * x)))


def _params(*sem):
    return pltpu.CompilerParams(dimension_semantics=sem, vmem_limit_bytes=VMEM_LIMIT)


def _full(shape):
    return pl.BlockSpec(shape, lambda *_: (0,) * len(shape))


def _mla_qkv_kernel(h_ref, g_ref, win_ref, qn_ref, kvn_ref, wq_ref, wkv_ref, c_ref, s_ref,
                    q_ref, k_ref, v_ref):
    hn = _rms(h_ref[...], g_ref[...]).astype(BF16)
    a = _dot(hn, win_ref[...])
    q_lat = a[:, :Q_LORA]
    kv_lat = a[:, Q_LORA:Q_LORA + KV_LORA]
    k_pe = a[:, Q_LORA + KV_LORA:Q_LORA + KV_LORA + HEAD_PAD]
    k_pe_rot = a[:, Q_LORA + KV_LORA + HEAD_PAD:]
    cos = c_ref[...]
    sin = s_ref[...]
    k_pe = k_pe * cos + k_pe_rot * sin
    qn = _rms(q_lat, qn_ref[...]).astype(BF16)
    kvn = _rms(kv_lat, kvn_ref[...]).astype(BF16)
    qq = _dot(qn, wq_ref[...])
    kv = _dot(kvn, wkv_ref[...])
    scale = 1.0 / math.sqrt(QK_NOPE + QK_ROPE)
    width = MLA_HEADS * HEAD_PAD
    for h in range(MLA_HEADS):
        lo, hi = h * HEAD_PAD, (h + 1) * HEAD_PAD
        q_h = (qq[:, lo:hi] * cos + qq[:, width + lo:width + hi] * sin) * scale
        q_ref[:, lo:hi] = q_h.astype(BF16)
        k_ref[:, lo:hi] = (kv[:, lo:hi] + k_pe).astype(BF16)
    v_ref[...] = kv[:, width:].astype(BF16)


def _mla_qkv(h, g, win, qn, kvn, wq, wkv, cos, sin):
    t = h.shape[0]
    d = h.shape[1]
    width = MLA_HEADS * HEAD_PAD
    rows = lambda w: pl.BlockSpec((ROW_BLOCK, w), lambda i: (i, 0))
    return pl.pallas_call(
        _mla_qkv_kernel,
        grid=(t // ROW_BLOCK,),
        in_specs=[rows(d), _full(g.shape), _full(win.shape), _full(qn.shape), _full(kvn.shape),
                  _full(wq.shape), _full(wkv.shape), rows(HEAD_PAD), rows(HEAD_PAD)],
        out_specs=[rows(width), rows(width), rows(MLA_HEADS * V_HEAD)],
        out_shape=[jax.ShapeDtypeStruct((t, width), BF16), jax.ShapeDtypeStruct((t, width), BF16),
                   jax.ShapeDtypeStruct((t, MLA_HEADS * V_HEAD), BF16)],
        compiler_params=_params("parallel"),
        name="mla_qkv",
    )(h, g, win, qn, kvn, wq, wkv, cos, sin)


def _attn_kernel(q_ref, k_ref, v_ref, qm_ref, km_ref, vm_ref, o_ref, om_ref, *, blk):
    qi = pl.program_id(1)
    q = q_ref[...]
    km = km_ref[...]
    vm = vm_ref[...]

    s = _dot_nt(q, km)
    m = jnp.max(s, axis=1, keepdims=True)
    p = jnp.exp(s - m)
    l = jnp.sum(p, axis=1, keepdims=True)
    acc = _dot(p.astype(BF16), vm)

    def step(j, carry, diagonal):
        m, l, acc = carry
        off = pl.multiple_of(j * blk, blk)
        s = _dot_nt(q, k_ref[pl.ds(off, blk), :])
        if diagonal:
            row = lax.broadcasted_iota(jnp.int32, (blk, blk), 0)
            col = lax.broadcasted_iota(jnp.int32, (blk, blk), 1)
            s = jnp.where(col <= row, s, -1e30)
        m_new = jnp.maximum(m, jnp.max(s, axis=1, keepdims=True))
        alpha = jnp.exp(m - m_new)
        p = jnp.exp(s - m_new)
        l = alpha * l + jnp.sum(p, axis=1, keepdims=True)
        acc = alpha * acc + _dot(p.astype(BF16), v_ref[pl.ds(off, blk), :])
        return m_new, l, acc

    carry = lax.fori_loop(0, qi, lambda j, c: step(j, c, False), (m, l, acc))
    m, l, acc = step(qi, carry, True)
    o_ref[...] = (acc / l).astype(BF16)

    @pl.when(qi == 0)
    def _meta_queries():
        s = _dot_nt(qm_ref[...], km)
        row = lax.broadcasted_iota(jnp.int32, (N_META, N_META), 0)
        col = lax.broadcasted_iota(jnp.int32, (N_META, N_META), 1)
        s = jnp.where(col <= row, s, -1e30)
        p = jnp.exp(s - jnp.max(s, axis=1, keepdims=True))
        o = _dot(p.astype(BF16), vm) / jnp.sum(p, axis=1, keepdims=True)
        om_ref[...] = o.astype(BF16)


def _attention(q, k, v, nb, seq):
    width = MLA_HEADS * HEAD_PAD
    vwidth = MLA_HEADS * V_HEAD
    steps = seq + N_META
    q2 = q.reshape(steps, nb * width)
    k2 = k.reshape(steps, nb * width)
    v2 = v.reshape(steps, nb * vwidth)
    blk = min(ATTN_BLOCK, seq)
    meta_blk = seq // N_META
    o, om = pl.pallas_call(
        functools.partial(_attn_kernel, blk=blk),
        grid=(nb * MLA_HEADS, seq // blk),
        in_specs=[pl.BlockSpec((blk, HEAD_PAD), lambda bh, qi: (qi, bh)),
                  pl.BlockSpec((seq, HEAD_PAD), lambda bh, qi: (0, bh)),
                  pl.BlockSpec((seq, V_HEAD), lambda bh, qi: (0, bh)),
                  pl.BlockSpec((N_META, HEAD_PAD), lambda bh, qi: (meta_blk, bh)),
                  pl.BlockSpec((N_META, HEAD_PAD), lambda bh, qi: (meta_blk, bh)),
                  pl.BlockSpec((N_META, V_HEAD), lambda bh, qi: (meta_blk, bh))],
        out_specs=[pl.BlockSpec((blk, V_HEAD), lambda bh, qi: (qi, bh)),
                   pl.BlockSpec((N_META, V_HEAD), lambda bh, qi: (0, bh))],
        out_shape=[jax.ShapeDtypeStruct((seq, nb * vwidth), BF16),
                   jax.ShapeDtypeStruct((N_META, nb * vwidth), BF16)],
        compiler_params=_params("parallel", "arbitrary"),
        name="mla_attention",
    )(q2, k2, v2, q2, k2, v2)
    return jnp.concatenate([o.reshape(seq * nb, vwidth), om.reshape(N_META * nb, vwidth)], axis=0)


def _proj_res_kernel(o_ref, w_ref, h_ref, out_ref):
    out_ref[...] = h_ref[...] + _dot(o_ref[...], w_ref[...])


def _proj_res(o, w, h):
    t, d = h.shape
    rows = lambda width: pl.BlockSpec((ROW_BLOCK, width), lambda i: (i, 0))
    return pl.pallas_call(
        _proj_res_kernel,
        grid=(t // ROW_BLOCK,),
        in_specs=[rows(o.shape[1]), _full(w.shape), rows(d)],
        out_specs=rows(d),
        out_shape=jax.ShapeDtypeStruct((t, d), F32),
        compiler_params=_params("parallel"),
        name="mla_out_proj",
    )(o, w, h)


def _lru_kernel(h_ref, g_ref, win_ref, cw_ref, cb_ref, wa_ref, ba_ref, wx_ref, bx_ref, lam_ref,
                wout_ref, out_ref, xbuf, state, *, nb):
    rows, width = h_ref.shape
    hist = (CONV_W - 1) * nb

    @pl.when(pl.program_id(0) == 0)
    def _init():
        xbuf[0:hist, :] = jnp.zeros((hist, width), F32)
        state[...] = jnp.zeros_like(state)

    x = h_ref[...]
    hn = _rms(x, g_ref[...]).astype(BF16)
    proj = _dot(hn, win_ref[...])
    gate = _gelu(proj[:, :width])
    xr = proj[:, width:]
    xbuf[hist:hist + rows, :] = xr
    xc = cb_ref[...] + xr * cw_ref[CONV_W - 1:CONV_W, :]
    for j in range(CONV_W - 1):
        xc = xc + xbuf[j * nb:j * nb + rows, :] * cw_ref[j:j + 1, :]
    xbuf[0:hist, :] = xbuf[rows:rows + hist, :]

    xcb = xc.astype(BF16)
    ra, ix = [], []
    for n in range(LRU_BLOCKS):
        blk = xcb[:, n * LRU_BW:(n + 1) * LRU_BW]
        ra.append(_dot(blk, wa_ref[n]))
        ix.append(_dot(blk, wx_ref[n]))
    r = jax.nn.sigmoid(jnp.concatenate(ra, axis=1) + ba_ref[...])
    i = jax.nn.sigmoid(jnp.concatenate(ix, axis=1) + bx_ref[...])
    lam = lam_ref[...]
    softplus = jnp.maximum(-lam, 0.0) + jnp.log1p(jnp.exp(-jnp.abs(lam)))
    log_a = -LRU_C * r * softplus
    a = jnp.exp(log_a)
    th = jnp.tanh(log_a)
    b = jnp.sqrt(-2.0 * th / (1.0 - th)) * (i * xc)

    hcur = state[...]
    hs = []
    for s in range(rows // nb):
        hcur = a[s * nb:(s + 1) * nb] * hcur + b[s * nb:(s + 1) * nb]
        hs.append(hcur)
    state[...] = hcur
    y = (jnp.concatenate(hs, axis=0) * gate).astype(BF16)
    out_ref[...] = x + _dot(y, wout_ref[...])


def _lru_block(h, g, win, cw, cb, wa, ba, wx, bx, lam, wout, nb):
    t, d = h.shape
    nblk = t // ROW_BLOCK
    rows = pl.BlockSpec((ROW_BLOCK, d), lambda i: ((i + nblk - 1) % nblk, 0))
    args = (g, win, cw, cb, wa, ba, wx, bx, lam, wout)
    return pl.pallas_call(
        functools.partial(_lru_kernel, nb=nb),
        grid=(nblk,),
        in_specs=[rows] + [_full(a.shape) for a in args],
        out_specs=rows,
        out_shape=jax.ShapeDtypeStruct((t, d), F32),
        scratch_shapes=[pltpu.VMEM(((CONV_W - 1) * nb + ROW_BLOCK, d), F32), pltpu.VMEM((nb, d), F32)],
        compiler_params=_params("arbitrary"),
        name="rglru_block",
    )(h, *args)


LANE_TILE = 256
ROW_TILE = 32


def _top_values(work, count, emit):
    for rank in range(count):
        mx = jnp.max(work, axis=0, keepdims=True)
        emit(rank, mx)
        if rank + 1 < count:
            work = jnp.where(work == mx, NEG_INF, work)


def _peer_kernel(h_ref, g_ref, wqt_ref, sk_ref, u_ref, vt_ref, out_ref,
                 xn_s, s_s, vals_s, e1_s, r0_s, e0_s, rows_s, st_s, wt_s, acc_s):
    c = pl.program_id(1)
    tb = h_ref.shape[0]
    nsub = u_ref.shape[0] // PEER_SUB
    per_sub = PEER_SUB // N_KEYS
    ntop = PEER_TOPK + 1
    vrows = vals_s.shape[1]
    lane_tiles = [slice(k * LANE_TILE, (k + 1) * LANE_TILE) for k in range(tb // LANE_TILE)]

    @pl.when(c == 0)
    def _prepare():
        xn_s[...] = _rms(h_ref[...], g_ref[...]).T.astype(BF16)

        def sub_scores(hp, carry):
            w = wqt_ref[pl.ds(pl.multiple_of(hp * N_KEYS, N_KEYS), N_KEYS), :]
            qt = _dot(w, xn_s[...]).astype(BF16)
            s_s[hp] = _dot(sk_ref[hp % 2], qt)
            for ls in lane_tiles:
                def emit(rank, mx):
                    vals_s[hp, rank:rank + 1, ls] = mx
                _top_values(s_s[hp, :, ls], ntop, emit)
                vals_s[hp, ntop:vrows, ls] = jnp.full((vrows - ntop, LANE_TILE), NEG_INF, F32)
            return carry

        lax.fori_loop(0, 2 * PEER_HEADS, sub_scores, 0)

        def head_gates(h, carry):
            for ls in lane_tiles:
                v0 = vals_s[2 * h, :, ls]
                v1 = vals_s[2 * h + 1, :, ls]
                cand = [v0[0:1] + v1] + [v0[a:a + 1] + v1[0:8] for a in range(1, 8)] + [v0[8:24] + v1[0:1]]
                got = {}

                def emit(rank, mx):
                    got[rank] = mx
                _top_values(jnp.concatenate(cand, axis=0), ntop, emit)
                z = sum(jnp.exp(got[k] - got[0]) for k in range(PEER_TOPK))
                thr = 0.5 * (got[PEER_TOPK - 1] + got[PEER_TOPK])
                s0 = s_s[2 * h, :, ls]
                e1_s[h, :, ls] = jnp.exp(s_s[2 * h + 1, :, ls] - v1[0:1])
                e0_s[h, :, ls] = jnp.exp(s0 - v0[0:1]) * (0.5 / z)
                r0_s[h, :, ls] = thr - s0
            return carry

        lax.fori_loop(0, PEER_HEADS, head_gates, 0)
        acc_s[...] = jnp.zeros_like(acc_s)

    def scores(sub):
        st_s[sub % 2] = _dot(u_ref[sub * PEER_SUB:(sub + 1) * PEER_SUB, :], xn_s[...])

    def gates(sub):
        buf = sub % 2
        i0 = (c * nsub + sub) * per_sub
        for h in range(PEER_HEADS):
            for ii in range(per_sub):
                rows_s[buf, h, ii, 0] = jnp.broadcast_to(r0_s[h, pl.ds(i0 + ii, 1), :], (8, tb))
                rows_s[buf, h, ii, 1] = jnp.broadcast_to(e0_s[h, pl.ds(i0 + ii, 1), :], (8, tb))
        for ls in lane_tiles:
            for jt in range(N_KEYS // ROW_TILE):
                js = slice(jt * ROW_TILE, (jt + 1) * ROW_TILE)
                gate = [jnp.zeros((ROW_TILE, LANE_TILE), F32) for _ in range(per_sub)]
                for h in range(PEER_HEADS):
                    s1 = s_s[2 * h + 1, js, ls]
                    e1 = e1_s[h, js, ls]
                    for ii in range(per_sub):
                        r = jnp.tile(rows_s[buf, h, ii, 0, :, ls], (ROW_TILE // 8, 1))
                        e = jnp.tile(rows_s[buf, h, ii, 1, :, ls], (ROW_TILE // 8, 1))
                        gate[ii] = gate[ii] + e * jnp.where(s1 >= r, e1, 0.0)
                for ii in range(per_sub):
                    rs = slice(ii * N_KEYS + jt * ROW_TILE, ii * N_KEYS + (jt + 1) * ROW_TILE)
                    s = st_s[buf, rs, ls]
                    act = s + s * jnp.tanh(s * (GELU_K0 + GELU_K1 * s * s))
                    wt_s[buf, rs, ls] = (act * gate[ii]).astype(BF16)

    scores(0)
    for sub in range(nsub):
        if sub + 1 < nsub:
            scores(sub + 1)
        gates(sub)
        acc_s[...] += _dot(vt_ref[:, sub * PEER_SUB:(sub + 1) * PEER_SUB], wt_s[sub % 2])

    @pl.when(c == pl.num_programs(1) - 1)
    def _finish():
        out_ref[...] = h_ref[...] + acc_s[...].T


def _peer(h, g, wqt, sk, u, vt):
    t, d = h.shape
    tb = PEER_TOKENS
    n_exp = u.shape[0]
    vrows = 24
    return pl.pallas_call(
        _peer_kernel,
        grid=(t // tb, n_exp // PEER_EXPERTS),
        in_specs=[pl.BlockSpec((tb, d), lambda i, c: (i, 0), pipeline_mode=pl.Buffered(1)),
                  _full(g.shape),
                  pl.BlockSpec(wqt.shape, lambda i, c: (0, 0), pipeline_mode=pl.Buffered(1)),
                  _full(sk.shape),
                  pl.BlockSpec((PEER_EXPERTS, d), lambda i, c: (c, 0)),
                  pl.BlockSpec((d, PEER_EXPERTS), lambda i, c: (0, c))],
        out_specs=pl.BlockSpec((tb, d), lambda i, c: (i, 0)),
        out_shape=jax.ShapeDtypeStruct((t, d), F32),
        scratch_shapes=[pltpu.VMEM((d, tb), BF16),
                        pltpu.VMEM((2 * PEER_HEADS, N_KEYS, tb), F32),
                        pltpu.VMEM((2 * PEER_HEADS, vrows, tb), F32),
                        pltpu.VMEM((PEER_HEADS, N_KEYS, tb), F32),
                        pltpu.VMEM((PEER_HEADS, N_KEYS, tb), F32),
                        pltpu.VMEM((PEER_HEADS, N_KEYS, tb), F32),
                        pltpu.VMEM((2, PEER_HEADS, PEER_SUB // N_KEYS, 2, 8, tb), F32),
                        pltpu.VMEM((2, PEER_SUB, tb), F32),
                        pltpu.VMEM((2, PEER_SUB, tb), BF16),
                        pltpu.VMEM((d, tb), F32)],
        compiler_params=_params("parallel", "arbitrary"),
        name="peer_ffn",
    )(h, g, wqt, sk, u, vt)


def _final_norm_kernel(h_ref, g_ref, out_ref):
    out_ref[...] = _rms(h_ref[...], g_ref[...])


def _final_norm(h, g, rows_out):
    d = h.shape[1]
    rows = pl.BlockSpec((ROW_BLOCK, d), lambda i: (i, 0))
    return pl.pallas_call(
        _final_norm_kernel,
        grid=(rows_out // ROW_BLOCK,),
        in_specs=[rows, _full(g.shape)],
        out_specs=rows,
        out_shape=jax.ShapeDtypeStruct((rows_out, d), F32),
        compiler_params=_params("parallel"),
        name="final_norm",
    )(h, g)


def _rot_cols(w):
    half = QK_ROPE // 2
    return jnp.concatenate([-w[:, half:], w[:, :half]], axis=1)


def _pad_rope(w):
    z = jnp.zeros((w.shape[0], QK_NOPE), w.dtype)
    z2 = jnp.zeros((w.shape[0], HEAD_PAD - QK_NOPE - QK_ROPE), w.dtype)
    return jnp.concatenate([z, w, z2], axis=1)


def _mla_weights(w_in, w_q_up, w_kv_up):
    w_kr = w_in[:, Q_LORA + KV_LORA:]
    win = jnp.concatenate([w_in[:, :Q_LORA + KV_LORA], _pad_rope(w_kr), _pad_rope(_rot_cols(w_kr))], axis=1)
    wq = w_q_up.reshape(Q_LORA, MLA_HEADS, QK_NOPE + QK_ROPE)
    pad = jnp.zeros((Q_LORA, MLA_HEADS, HEAD_PAD - QK_NOPE - QK_ROPE), w_q_up.dtype)
    nope0 = jnp.zeros((Q_LORA, MLA_HEADS, QK_NOPE), w_q_up.dtype)
    rope = wq[:, :, QK_NOPE:]
    half = QK_ROPE // 2
    rope_rot = jnp.concatenate([-rope[:, :, half:], rope[:, :, :half]], axis=2)
    q_plain = jnp.concatenate([wq, pad], axis=2).reshape(Q_LORA, MLA_HEADS * HEAD_PAD)
    q_rot = jnp.concatenate([nope0, rope_rot, pad], axis=2).reshape(Q_LORA, MLA_HEADS * HEAD_PAD)
    wkv = w_kv_up.reshape(KV_LORA, MLA_HEADS, QK_NOPE + V_HEAD)
    kpad = jnp.zeros((KV_LORA, MLA_HEADS, HEAD_PAD - QK_NOPE), w_kv_up.dtype)
    k_nope = jnp.concatenate([wkv[:, :, :QK_NOPE], kpad], axis=2).reshape(KV_LORA, MLA_HEADS * HEAD_PAD)
    v = wkv[:, :, QK_NOPE:].reshape(KV_LORA, MLA_HEADS * V_HEAD)
    return (win.astype(BF16), jnp.concatenate([q_plain, q_rot], axis=1).astype(BF16),
            jnp.concatenate([k_nope, v], axis=1).astype(BF16))


def _rope_tables(pos):
    half = QK_ROPE // 2
    freqs = ROPE_THETA ** (-jnp.arange(half, dtype=F32) / half)
    ang = pos.astype(F32)[:, None] * freqs
    n = pos.shape[0]
    ones = jnp.ones((n, QK_NOPE), F32)
    zeros = jnp.zeros((n, QK_NOPE), F32)
    tail = jnp.zeros((n, HEAD_PAD - QK_NOPE - QK_ROPE), F32)
    cos = jnp.concatenate([ones, jnp.cos(ang), jnp.cos(ang), tail], axis=1)
    sin = jnp.concatenate([zeros, jnp.sin(ang), jnp.sin(ang), tail], axis=1)
    return cos, sin


def kernel(x, positions, meta_tokens, mixer_norm, ffn_norm, final_norm, mla_w_in, mla_q_norm, mla_w_q_up, mla_kv_norm, mla_w_kv_up, mla_w_out, lru_w_in, lru_conv_w, lru_conv_b, lru_w_gate_a, lru_b_gate_a, lru_w_gate_x, lru_b_gate_x, lru_lambda, lru_w_out, peer_w_query, peer_sub_keys, peer_u, peer_v):
    nb, seq, d = x.shape
    assert nb % 8 == 0 and (nb * N_META) % ROW_BLOCK == 0 and (nb * seq) % ROW_BLOCK == 0
    assert (nb * (seq + N_META)) % PEER_TOKENS == 0 and seq % min(ATTN_BLOCK, seq) == 0
    row = lambda v: v.reshape(1, -1).astype(F32)

    h = jnp.concatenate([x.transpose(1, 0, 2).reshape(seq * nb, d),
                         jnp.repeat(meta_tokens.astype(x.dtype), nb, axis=0)], axis=0)
    pos = jnp.concatenate([positions.astype(jnp.int32).T.reshape(-1) + N_META,
                           jnp.repeat(jnp.arange(N_META, dtype=jnp.int32), nb)], axis=0)
    cos, sin = _rope_tables(pos)

    def peer(h, layer):
        wqt = peer_w_query[layer].T.astype(BF16)
        return _peer(h, row(ffn_norm[layer]), wqt, peer_sub_keys[layer].astype(BF16),
                     peer_u[layer].astype(BF16), peer_v[layer].T.astype(BF16))

    win, wq, wkv = _mla_weights(mla_w_in[0], mla_w_q_up[0], mla_w_kv_up[0])
    q, k, v = _mla_qkv(h, row(mixer_norm[0]), win, row(mla_q_norm[0]), row(mla_kv_norm[0]), wq, wkv, cos, sin)
    o = _attention(q, k, v, nb, seq)
    h = _proj_res(o, mla_w_out[0].astype(BF16), h)
    h = peer(h, 0)

    h = _lru_block(h, row(mixer_norm[1]), lru_w_in[0].astype(BF16), lru_conv_w[0].astype(F32),
                   row(lru_conv_b[0]), lru_w_gate_a[0].astype(BF16), row(lru_b_gate_a[0]),
                   lru_w_gate_x[0].astype(BF16), row(lru_b_gate_x[0]), row(lru_lambda[0]),
                   lru_w_out[0].astype(BF16), nb)
    h = peer(h, 1)

    out = _final_norm(h, row(final_norm), seq * nb)
    return out.reshape(seq, nb, d).transpose(1, 0, 2)
```

```python
import functools
import math

import jax
import jax.numpy as jnp
from jax import lax
from jax.experimental import pallas as pl
from jax.experimental.pallas import tpu as pltpu

F32 = jnp.float32
BF16 = jnp.bfloat16

N_META = 16
NORM_EPS = 1e-6
MLA_HEADS = 8
QK_NOPE = 128
QK_ROPE = 64
V_HEAD = 128
Q_LORA = 384
KV_LORA = 256
ROPE_THETA = 10000.0
HEAD_PAD = 256
LRU_BLOCKS = 8
LRU_BW = 128
CONV_W = 4
LRU_C = 8.0
PEER_HEADS = 8
N_KEYS = 128
PEER_TOPK = 16
GELU_K0 = math.sqrt(2.0 / math.pi)
GELU_K1 = GELU_K0 * 0.044715
NEG_INF = float("-inf")

ROW_BLOCK = 256
ATTN_BLOCK = 512
PEER_TOKENS = 768
PEER_EXPERTS = 2048
PEER_SUB = 256
VMEM_LIMIT = 58 * 1024 * 1024


def _dot(a, b):
    return jnp.dot(a, b, preferred_element_type=F32)


def _dot_nt(a, b):
    return lax.dot_general(a, b, (((1,), (1,)), ((), ())), preferred_element_type=F32)


def _rms(x, g):
    return x * lax.rsqrt(jnp.mean(x * x, axis=-1, keepdims=True) + NORM_EPS) * g


def _gelu(x):
    return 0.5 * x * (1.0 + jnp.tanh(x * (GELU_K0 + GELU_K1 * x * x)))


def _params(*sem):
    return pltpu.CompilerParams(dimension_semantics=sem, vmem_limit_bytes=VMEM_LIMIT)


def _full(shape):
    return pl.BlockSpec(shape, lambda *_: (0,) * len(shape))


def _mla_qkv_kernel(h_ref, g_ref, win_ref, qn_ref, kvn_ref, wq_ref, wkv_ref, c_ref, s_ref,
                    q_ref, k_ref, v_ref):
    hn = _rms(h_ref[...], g_ref[...]).astype(BF16)
    a = _dot(hn, win_ref[...])
    q_lat = a[:, :Q_LORA]
    kv_lat = a[:, Q_LORA:Q_LORA + KV_LORA]
    k_pe = a[:, Q_LORA + KV_LORA:Q_LORA + KV_LORA + HEAD_PAD]
    k_pe_rot = a[:, Q_LORA + KV_LORA + HEAD_PAD:]
    cos = c_ref[...]
    sin = s_ref[...]
    k_pe = k_pe * cos + k_pe_rot * sin
    qn = _rms(q_lat, qn_ref[...]).astype(BF16)
    kvn = _rms(kv_lat, kvn_ref[...]).astype(BF16)
    qq = _dot(qn, wq_ref[...])
    kv = _dot(kvn, wkv_ref[...])
    scale = 1.0 / math.sqrt(QK_NOPE + QK_ROPE)
    width = MLA_HEADS * HEAD_PAD
    for h in range(MLA_HEADS):
        lo, hi = h * HEAD_PAD, (h + 1) * HEAD_PAD
        q_h = (qq[:, lo:hi] * cos + qq[:, width + lo:width + hi] * sin) * scale
        q_ref[:, lo:hi] = q_h.astype(BF16)
        k_ref[:, lo:hi] = (kv[:, lo:hi] + k_pe).astype(BF16)
    v_ref[...] = kv[:, width:].astype(BF16)


def _mla_qkv(h, g, win, qn, kvn, wq, wkv, cos, sin):
    t = h.shape[0]
    d = h.shape[1]
    width = MLA_HEADS * HEAD_PAD
    rows = lambda w: pl.BlockSpec((ROW_BLOCK, w), lambda i: (i, 0))
    return pl.pallas_call(
        _mla_qkv_kernel,
        grid=(t // ROW_BLOCK,),
        in_specs=[rows(d), _full(g.shape), _full(win.shape), _full(qn.shape), _full(kvn.shape),
                  _full(wq.shape), _full(wkv.shape), rows(HEAD_PAD), rows(HEAD_PAD)],
        out_specs=[rows(width), rows(width), rows(MLA_HEADS * V_HEAD)],
        out_shape=[jax.ShapeDtypeStruct((t, width), BF16), jax.ShapeDtypeStruct((t, width), BF16),
                   jax.ShapeDtypeStruct((t, MLA_HEADS * V_HEAD), BF16)],
        compiler_params=_params("parallel"),
        name="mla_qkv",
    )(h, g, win, qn, kvn, wq, wkv, cos, sin)


def _attn_kernel(q_ref, k_ref, v_ref, qm_ref, km_ref, vm_ref, o_ref, om_ref, *, blk):
    qi = pl.program_id(1)
    q = q_ref[...]
    km = km_ref[...]
    vm = vm_ref[...]

    s = _dot_nt(q, km)
    m = jnp.max(s, axis=1, keepdims=True)
    p = jnp.exp(s - m)
    l = jnp.sum(p, axis=1, keepdims=True)
    acc = _dot(p.astype(BF16), vm)

    def step(j, carry, diagonal):
        m, l, acc = carry
        off = pl.multiple_of(j * blk, blk)
        s = _dot_nt(q, k_ref[pl.ds(off, blk), :])
        if diagonal:
            row = lax.broadcasted_iota(jnp.int32, (blk, blk), 0)
            col = lax.broadcasted_iota(jnp.int32, (blk, blk), 1)
            s = jnp.where(col <= row, s, -1e30)
        m_new = jnp.maximum(m, jnp.max(s, axis=1, keepdims=True))
        alpha = jnp.exp(m - m_new)
        p = jnp.exp(s - m_new)
        l = alpha * l + jnp.sum(p, axis=1, keepdims=True)
        acc = alpha * acc + _dot(p.astype(BF16), v_ref[pl.ds(off, blk), :])
        return m_new, l, acc

    carry = lax.fori_loop(0, qi, lambda j, c: step(j, c, False), (m, l, acc))
    m, l, acc = step(qi, carry, True)
    o_ref[...] = (acc / l).astype(BF16)

    @pl.when(qi == 0)
    def _meta_queries():
        s = _dot_nt(qm_ref[...], km)
        row = lax.broadcasted_iota(jnp.int32, (N_META, N_META), 0)
        col = lax.broadcasted_iota(jnp.int32, (N_META, N_META), 1)
        s = jnp.where(col <= row, s, -1e30)
        p = jnp.exp(s - jnp.max(s, axis=1, keepdims=True))
        o = _dot(p.astype(BF16), vm) / jnp.sum(p, axis=1, keepdims=True)
        om_ref[...] = o.astype(BF16)


def _attention(q, k, v, nb, seq):
    width = MLA_HEADS * HEAD_PAD
    vwidth = MLA_HEADS * V_HEAD
    steps = seq + N_META
    q2 = q.reshape(steps, nb * width)
    k2 = k.reshape(steps, nb * width)
    v2 = v.reshape(steps, nb * vwidth)
    blk = min(ATTN_BLOCK, seq)
    meta_blk = seq // N_META
    o, om = pl.pallas_call(
        functools.partial(_attn_kernel, blk=blk),
        grid=(nb * MLA_HEADS, seq // blk),
        in_specs=[pl.BlockSpec((blk, HEAD_PAD), lambda bh, qi: (qi, bh)),
                  pl.BlockSpec((seq, HEAD_PAD), lambda bh, qi: (0, bh)),
                  pl.BlockSpec((seq, V_HEAD), lambda bh, qi: (0, bh)),
                  pl.BlockSpec((N_META, HEAD_PAD), lambda bh, qi: (meta_blk, bh)),
                  pl.BlockSpec((N_META, HEAD_PAD), lambda bh, qi: (meta_blk, bh)),
                  pl.BlockSpec((N_META, V_HEAD), lambda bh, qi: (meta_blk, bh))],
        out_specs=[pl.BlockSpec((blk, V_HEAD), lambda bh, qi: (qi, bh)),
                   pl.BlockSpec((N_META, V_HEAD), lambda bh, qi: (0, bh))],
        out_shape=[jax.ShapeDtypeStruct((seq, nb * vwidth), BF16),
                   jax.ShapeDtypeStruct((N_META, nb * vwidth), BF16)],
        compiler_params=_params("parallel", "arbitrary"),
        name="mla_attention",
    )(q2, k2, v2, q2, k2, v2)
    return jnp.concatenate([o.reshape(seq * nb, vwidth), om.reshape(N_META * nb, vwidth)], axis=0)


def _proj_res_kernel(o_ref, w_ref, h_ref, out_ref):
    out_ref[...] = h_ref[...] + _dot(o_ref[...], w_ref[...])


def _proj_res(o, w, h):
    t, d = h.shape
    rows = lambda width: pl.BlockSpec((ROW_BLOCK, width), lambda i: (i, 0))
    return pl.pallas_call(
        _proj_res_kernel,
        grid=(t // ROW_BLOCK,),
        in_specs=[rows(o.shape[1]), _full(w.shape), rows(d)],
        out_specs=rows(d),
        out_shape=jax.ShapeDtypeStruct((t, d), F32),
        compiler_params=_params("parallel"),
        name="mla_out_proj",
    )(o, w, h)


def _lru_kernel(h_ref, g_ref, win_ref, cw_ref, cb_ref, wa_ref, ba_ref, wx_ref, bx_ref, lam_ref,
                wout_ref, out_ref, xbuf, state, *, nb):
    rows, width = h_ref.shape
    hist = (CONV_W - 1) * nb

    @pl.when(pl.program_id(0) == 0)
    def _init():
        xbuf[0:hist, :] = jnp.zeros((hist, width), F32)
        state[...] = jnp.zeros_like(state)

    x = h_ref[...]
    hn = _rms(x, g_ref[...]).astype(BF16)
    proj = _dot(hn, win_ref[...])
    gate = _gelu(proj[:, :width])
    xr = proj[:, width:]
    xbuf[hist:hist + rows, :] = xr
    xc = cb_ref[...] + xr * cw_ref[CONV_W - 1:CONV_W, :]
    for j in range(CONV_W - 1):
        xc = xc + xbuf[j * nb:j * nb + rows, :] * cw_ref[j:j + 1, :]
    xbuf[0:hist, :] = xbuf[rows:rows + hist, :]

    xcb = xc.astype(BF16)
    ra, ix = [], []
    for n in range(LRU_BLOCKS):
        blk = xcb[:, n * LRU_BW:(n + 1) * LRU_BW]
        ra.append(_dot(blk, wa_ref[n]))
        ix.append(_dot(blk, wx_ref[n]))
    r = jax.nn.sigmoid(jnp.concatenate(ra, axis=1) + ba_ref[...])
    i = jax.nn.sigmoid(jnp.concatenate(ix, axis=1) + bx_ref[...])
    lam = lam_ref[...]
    softplus = jnp.maximum(-lam, 0.0) + jnp.log1p(jnp.exp(-jnp.abs(lam)))
    log_a = -LRU_C * r * softplus
    a = jnp.exp(log_a)
    th = jnp.tanh(log_a)
    b = jnp.sqrt(-2.0 * th / (1.0 - th)) * (i * xc)

    hcur = state[...]
    hs = []
    for s in range(rows // nb):
        hcur = a[s * nb:(s + 1) * nb] * hcur + b[s * nb:(s + 1) * nb]
        hs.append(hcur)
    state[...] = hcur
    y = (jnp.concatenate(hs, axis=0) * gate).astype(BF16)
    out_ref[...] = x + _dot(y, wout_ref[...])


def _lru_block(h, g, win, cw, cb, wa, ba, wx, bx, lam, wout, nb):
    t, d = h.shape
    nblk = t // ROW_BLOCK
    rows = pl.BlockSpec((ROW_BLOCK, d), lambda i: ((i + nblk - 1) % nblk, 0))
    args = (g, win, cw, cb, wa, ba, wx, bx, lam, wout)
    return pl.pallas_call(
        functools.partial(_lru_kernel, nb=nb),
        grid=(nblk,),
        in_specs=[rows] + [_full(a.shape) for a in args],
        out_specs=rows,
        out_shape=jax.ShapeDtypeStruct((t, d), F32),
        scratch_shapes=[pltpu.VMEM(((CONV_W - 1) * nb + ROW_BLOCK, d), F32), pltpu.VMEM((nb, d), F32)],
        compiler_params=_params("arbitrary"),
        name="rglru_block",
    )(h, *args)


LANE_TILE = 256
ROW_TILE = 64
RANK_CODE_BASE = 1024.0
RANK_CODE_SCALE = 2.0 ** 90
RANK_NONE = 31.0


def _col_max(tiles):
    while len(tiles) > 1:
        tiles = [jnp.maximum(a, b) for a, b in zip(tiles[0::2], tiles[1::2])] + tiles[len(tiles) & ~1:]
    m = tiles[0]
    for shift in (4, 2, 1):
        m = jnp.maximum(m, pltpu.roll(m, shift, axis=0))
    return m


def _extract_top(tiles, count, emit, rank_codes=False):
    for rank in range(count):
        m = _col_max(tiles)
        emit(rank, m)
        if rank_codes:
            code = -(RANK_CODE_BASE + rank) * RANK_CODE_SCALE
            tiles = [jnp.where(t == m, code, t) for t in tiles]
        elif rank + 1 < count:
            tiles = [jnp.where(t == m, NEG_INF, t) for t in tiles]
    return tiles


def _pair_words(x):
    bits = pltpu.bitcast(x, jnp.uint32)
    return bits | (bits >> 16)


def _peer_kernel(h_ref, g_ref, wqt_ref, sk_ref, u_ref, vt_ref, out_ref,
                 xn_s, s_s, vals_s, rep_s, rk_s, e1_s, n_s, e0_s, rows_s, st_s, wt_s, acc_s):
    c = pl.program_id(1)
    tb = h_ref.shape[0]
    nsub = u_ref.shape[0] // PEER_SUB
    per_sub = PEER_SUB // N_KEYS
    ntop = PEER_TOPK + 1
    vrows = vals_s.shape[1]
    key_tiles = N_KEYS // 8
    lane_tiles = [slice(k * LANE_TILE, (k + 1) * LANE_TILE) for k in range(tb // LANE_TILE)]

    @pl.when(c == 0)
    def _prepare():
        xn_s[...] = _rms(h_ref[...], g_ref[...]).T.astype(BF16)

        def head_tables(h, carry):
            for p in range(2):
                w = wqt_ref[pl.ds(pl.multiple_of((2 * h + p) * N_KEYS, N_KEYS), N_KEYS), :]
                s_s[p] = _dot(sk_ref[p], _dot(w, xn_s[...]).astype(BF16))
            for ls in lane_tiles:
                top0 = {}
                neg = jnp.full((vrows - ntop, LANE_TILE), NEG_INF, F32)

                def emit0(rank, m):
                    top0[rank] = m
                    vals_s[0, rank:rank + 1, ls] = m[0:1]

                def emit1(rank, m):
                    rep_s[rank, :, ls] = m
                    vals_s[1, rank:rank + 1, ls] = m[0:1]

                _extract_top([s_s[0, 8 * k:8 * k + 8, ls] for k in range(key_tiles)], ntop, emit0)
                coded = _extract_top([s_s[1, 8 * k:8 * k + 8, ls] for k in range(key_tiles)], ntop, emit1,
                                     rank_codes=True)
                vals_s[0, ntop:vrows, ls] = neg
                vals_s[1, ntop:vrows, ls] = neg
                v0 = vals_s[0, :, ls]
                v1 = vals_s[1, :, ls]
                cand = ([top0[0] + v1[8 * k:8 * k + 8] for k in range(vrows // 8)]
                        + [top0[a] + v1[0:8] for a in range(1, 8)]
                        + [v0[8 * k:8 * k + 8] + rep_s[0, :, ls] for k in range(1, vrows // 8)])
                got = {}

                def emit(rank, m):
                    got[rank] = m
                _extract_top(cand, ntop, emit)
                z = sum(jnp.exp(got[k] - got[0]) for k in range(PEER_TOPK))
                thr = 0.5 * (got[PEER_TOPK - 1] + got[PEER_TOPK])
                scale = 0.5 / z
                for k in range(0, key_tiles, 2):
                    rows = slice(8 * k, 8 * k + 16)
                    s0 = s_s[0, rows, ls]
                    slack = jnp.concatenate([thr, thr], axis=0) - s0
                    n = jnp.zeros_like(s0)
                    for b in range(ntop):
                        vb = rep_s[b, :, ls]
                        n = jnp.where(jnp.concatenate([vb, vb], axis=0) >= slack, float(b + 1), n)
                    n_s[h, rows, ls] = _pair_words(n)
                    m0 = jnp.concatenate([top0[0], top0[0]], axis=0)
                    e0 = jnp.exp(s0 - m0) * jnp.concatenate([scale, scale], axis=0)
                    e0_s[h, rows, ls] = _pair_words(e0.astype(BF16).astype(F32))
                    m1 = rep_s[0, :, ls]
                    e1 = jnp.exp(s_s[1, rows, ls] - jnp.concatenate([m1, m1], axis=0))
                    e1_s[h, rows, ls] = e1.astype(BF16)
                    t = jnp.concatenate([coded[k], coded[k + 1]], axis=0)
                    rank = jnp.where(t <= -0.5 * RANK_CODE_BASE * RANK_CODE_SCALE,
                                     t * (-1.0 / RANK_CODE_SCALE) - RANK_CODE_BASE, RANK_NONE)
                    rk_s[h, rows, ls] = rank.astype(BF16)
            return carry

        lax.fori_loop(0, PEER_HEADS, head_tables, 0)
        acc_s[...] = jnp.zeros_like(acc_s)

    def scores(sub):
        st_s[sub % 2] = _dot(u_ref[sub * PEER_SUB:(sub + 1) * PEER_SUB, :], xn_s[...])

    def gates(sub):
        buf = sub % 2
        i0 = (c * nsub + sub) * per_sub
        for h in range(PEER_HEADS):
            for ii in range(per_sub):
                rows_s[buf, h, ii, 0] = jnp.broadcast_to(n_s[h, pl.ds(i0 + ii, 1), :], (8, tb))
                rows_s[buf, h, ii, 1] = jnp.broadcast_to(e0_s[h, pl.ds(i0 + ii, 1), :], (8, tb))
        reps = ROW_TILE // 16
        for ls in lane_tiles:
            for jt in range(N_KEYS // ROW_TILE):
                js = slice(jt * ROW_TILE, (jt + 1) * ROW_TILE)
                gate = [jnp.zeros((ROW_TILE, LANE_TILE), BF16) for _ in range(per_sub)]
                for h in range(PEER_HEADS):
                    rk = rk_s[h, js, ls]
                    e1 = e1_s[h, js, ls]
                    for ii in range(per_sub):
                        n = jnp.tile(pltpu.bitcast(rows_s[buf, h, ii, 0, :, ls], BF16), (reps, 1))
                        e = jnp.tile(pltpu.bitcast(rows_s[buf, h, ii, 1, :, ls], BF16), (reps, 1))
                        gate[ii] = gate[ii] + e * jnp.where(rk < n, e1, jnp.zeros_like(e1))
                for ii in range(per_sub):
                    rs = slice(ii * N_KEYS + jt * ROW_TILE, ii * N_KEYS + (jt + 1) * ROW_TILE)
                    s = st_s[buf, rs, ls]
                    act = s + s * jnp.tanh(s * (GELU_K0 + GELU_K1 * s * s))
                    wt_s[buf, rs, ls] = act.astype(BF16) * gate[ii]

    scores(0)
    for sub in range(nsub):
        if sub + 1 < nsub:
            scores(sub + 1)
        gates(sub)
        acc_s[...] += _dot(vt_ref[:, sub * PEER_SUB:(sub + 1) * PEER_SUB], wt_s[sub % 2])

    @pl.when(c == pl.num_programs(1) - 1)
    def _finish():
        out_ref[...] = h_ref[...] + acc_s[...].T


def _peer(h, g, wqt, sk, u, vt):
    t, d = h.shape
    tb = PEER_TOKENS
    n_exp = u.shape[0]
    vrows = 24
    return pl.pallas_call(
        _peer_kernel,
        grid=(t // tb, n_exp // PEER_EXPERTS),
        in_specs=[pl.BlockSpec((tb, d), lambda i, c: (i, 0), pipeline_mode=pl.Buffered(1)),
                  _full(g.shape),
                  pl.BlockSpec(wqt.shape, lambda i, c: (0, 0), pipeline_mode=pl.Buffered(1)),
                  _full(sk.shape),
                  pl.BlockSpec((PEER_EXPERTS, d), lambda i, c: (c, 0)),
                  pl.BlockSpec((d, PEER_EXPERTS), lambda i, c: (0, c))],
        out_specs=pl.BlockSpec((tb, d), lambda i, c: (i, 0)),
        out_shape=jax.ShapeDtypeStruct((t, d), F32),
        scratch_shapes=[pltpu.VMEM((d, tb), BF16),
                        pltpu.VMEM((2, N_KEYS, tb), F32),
                        pltpu.VMEM((2, vrows, tb), F32),
                        pltpu.VMEM((vrows, 8, tb), F32),
                        pltpu.VMEM((PEER_HEADS, N_KEYS, tb), BF16),
                        pltpu.VMEM((PEER_HEADS, N_KEYS, tb), BF16),
                        pltpu.VMEM((PEER_HEADS, N_KEYS, tb), jnp.uint32),
                        pltpu.VMEM((PEER_HEADS, N_KEYS, tb), jnp.uint32),
                        pltpu.VMEM((2, PEER_HEADS, PEER_SUB // N_KEYS, 2, 8, tb), jnp.uint32),
                        pltpu.VMEM((2, PEER_SUB, tb), F32),
                        pltpu.VMEM((2, PEER_SUB, tb), BF16),
                        pltpu.VMEM((d, tb), F32)],
        compiler_params=_params("parallel", "arbitrary"),
        name="peer_ffn",
    )(h, g, wqt, sk, u, vt)


def _final_norm_kernel(h_ref, g_ref, out_ref):
    out_ref[...] = _rms(h_ref[...], g_ref[...])


def _final_norm(h, g, rows_out):
    d = h.shape[1]
    rows = pl.BlockSpec((ROW_BLOCK, d), lambda i: (i, 0))
    return pl.pallas_call(
        _final_norm_kernel,
        grid=(rows_out // ROW_BLOCK,),
        in_specs=[rows, _full(g.shape)],
        out_specs=rows,
        out_shape=jax.ShapeDtypeStruct((rows_out, d), F32),
        compiler_params=_params("parallel"),
        name="final_norm",
    )(h, g)


def _rot_cols(w):
    half = QK_ROPE // 2
    return jnp.concatenate([-w[:, half:], w[:, :half]], axis=1)


def _pad_rope(w):
    z = jnp.zeros((w.shape[0], QK_NOPE), w.dtype)
    z2 = jnp.zeros((w.shape[0], HEAD_PAD - QK_NOPE - QK_ROPE), w.dtype)
    return jnp.concatenate([z, w, z2], axis=1)


def _mla_weights(w_in, w_q_up, w_kv_up):
    w_kr = w_in[:, Q_LORA + KV_LORA:]
    win = jnp.concatenate([w_in[:, :Q_LORA + KV_LORA], _pad_rope(w_kr), _pad_rope(_rot_cols(w_kr))], axis=1)
    wq = w_q_up.reshape(Q_LORA, MLA_HEADS, QK_NOPE + QK_ROPE)
    pad = jnp.zeros((Q_LORA, MLA_HEADS, HEAD_PAD - QK_NOPE - QK_ROPE), w_q_up.dtype)
    nope0 = jnp.zeros((Q_LORA, MLA_HEADS, QK_NOPE), w_q_up.dtype)
    rope = wq[:, :, QK_NOPE:]
    half = QK_ROPE // 2
    rope_rot = jnp.concatenate([-rope[:, :, half:], rope[:, :, :half]], axis=2)
    q_plain = jnp.concatenate([wq, pad], axis=2).reshape(Q_LORA, MLA_HEADS * HEAD_PAD)
    q_rot = jnp.concatenate([nope0, rope_rot, pad], axis=2).reshape(Q_LORA, MLA_HEADS * HEAD_PAD)
    wkv = w_kv_up.reshape(KV_LORA, MLA_HEADS, QK_NOPE + V_HEAD)
    kpad = jnp.zeros((KV_LORA, MLA_HEADS, HEAD_PAD - QK_NOPE), w_kv_up.dtype)
    k_nope = jnp.concatenate([wkv[:, :, :QK_NOPE], kpad], axis=2).reshape(KV_LORA, MLA_HEADS * HEAD_PAD)
    v = wkv[:, :, QK_NOPE:].reshape(KV_LORA, MLA_HEADS * V_HEAD)
    return (win.astype(BF16), jnp.concatenate([q_plain, q_rot], axis=1).astype(BF16),
            jnp.concatenate([k_nope, v], axis=1).astype(BF16))


def _rope_tables(pos):
    half = QK_ROPE // 2
    freqs = ROPE_THETA ** (-jnp.arange(half, dtype=F32) / half)
    ang = pos.astype(F32)[:, None] * freqs
    n = pos.shape[0]
    ones = jnp.ones((n, QK_NOPE), F32)
    zeros = jnp.zeros((n, QK_NOPE), F32)
    tail = jnp.zeros((n, HEAD_PAD - QK_NOPE - QK_ROPE), F32)
    cos = jnp.concatenate([ones, jnp.cos(ang), jnp.cos(ang), tail], axis=1)
    sin = jnp.concatenate([zeros, jnp.sin(ang), jnp.sin(ang), tail], axis=1)
    return cos, sin


def kernel(x, positions, meta_tokens, mixer_norm, ffn_norm, final_norm, mla_w_in, mla_q_norm, mla_w_q_up, mla_kv_norm, mla_w_kv_up, mla_w_out, lru_w_in, lru_conv_w, lru_conv_b, lru_w_gate_a, lru_b_gate_a, lru_w_gate_x, lru_b_gate_x, lru_lambda, lru_w_out, peer_w_query, peer_sub_keys, peer_u, peer_v):
    nb, seq, d = x.shape
    assert nb % 8 == 0 and (nb * N_META) % ROW_BLOCK == 0 and (nb * seq) % ROW_BLOCK == 0
    assert (nb * (seq + N_META)) % PEER_TOKENS == 0 and seq % min(ATTN_BLOCK, seq) == 0
    row = lambda v: v.reshape(1, -1).astype(F32)

    h = jnp.concatenate([x.transpose(1, 0, 2).reshape(seq * nb, d),
                         jnp.repeat(meta_tokens.astype(x.dtype), nb, axis=0)], axis=0)
    pos = jnp.concatenate([positions.astype(jnp.int32).T.reshape(-1) + N_META,
                           jnp.repeat(jnp.arange(N_META, dtype=jnp.int32), nb)], axis=0)
    cos, sin = _rope_tables(pos)

    def peer(h, layer):
        wqt = peer_w_query[layer].T.astype(BF16)
        return _peer(h, row(ffn_norm[layer]), wqt, peer_sub_keys[layer].astype(BF16),
                     peer_u[layer].astype(BF16), peer_v[layer].T.astype(BF16))

    win, wq, wkv = _mla_weights(mla_w_in[0], mla_w_q_up[0], mla_w_kv_up[0])
    q, k, v = _mla_qkv(h, row(mixer_norm[0]), win, row(mla_q_norm[0]), row(mla_kv_norm[0]), wq, wkv, cos, sin)
    o = _attention(q, k, v, nb, seq)
    h = _proj_res(o, mla_w_out[0].astype(BF16), h)
    h = peer(h, 0)

    h = _lru_block(h, row(mixer_norm[1]), lru_w_in[0].astype(BF16), lru_conv_w[0].astype(F32),
                   row(lru_conv_b[0]), lru_w_gate_a[0].astype(BF16), row(lru_b_gate_a[0]),
                   lru_w_gate_x[0].astype(BF16), row(lru_b_gate_x[0]), row(lru_lambda[0]),
                   lru_w_out[0].astype(BF16), nb)
    h = peer(h, 1)

    out = _final_norm(h, row(final_norm), seq * nb)
    return out.reshape(seq, nb, d).transpose(1, 0, 2)
```

```python
import functools
import math

import jax
import jax.numpy as jnp
from jax import lax
from jax.experimental import pallas as pl
from jax.experimental.pallas import tpu as pltpu

F32 = jnp.float32
BF16 = jnp.bfloat16

N_META = 16
NORM_EPS = 1e-6
MLA_HEADS = 8
QK_NOPE = 128
QK_ROPE = 64
V_HEAD = 128
Q_LORA = 384
KV_LORA = 256
ROPE_THETA = 10000.0
HEAD_PAD = 256
LRU_BLOCKS = 8
LRU_BW = 128
CONV_W = 4
LRU_C = 8.0
PEER_HEADS = 8
N_KEYS = 128
PEER_TOPK = 16
GELU_K0 = math.sqrt(2.0 / math.pi)
GELU_K1 = GELU_K0 * 0.044715
NEG_INF = float("-inf")

ROW_BLOCK = 256
ATTN_BLOCK = 512
PEER_TOKENS = 768
PEER_EXPERTS = 2048
PEER_SUB = 256
VMEM_LIMIT = 58 * 1024 * 1024


def _dot(a, b):
    return jnp.dot(a, b, preferred_element_type=F32)


def _dot_nt(a, b):
    return lax.dot_general(a, b, (((1,), (1,)), ((), ())), preferred_element_type=F32)


def _rms(x, g):
    return x * lax.rsqrt(jnp.mean(x * x, axis=-1, keepdims=True) + NORM_EPS) * g


def _gelu(x):
    return 0.5 * x * (1.0 + jnp.tanh(x * (GELU_K0 + GELU_K1 * x * x)))


def _params(*sem):
    return pltpu.CompilerParams(dimension_semantics=sem, vmem_limit_bytes=VMEM_LIMIT)


def _full(shape):
    return pl.BlockSpec(shape, lambda *_: (0,) * len(shape))


def _mla_qkv_kernel(h_ref, g_ref, win_ref, qn_ref, kvn_ref, wq_ref, wkv_ref, c_ref, s_ref,
                    q_ref, k_ref, v_ref):
    hn = _rms(h_ref[...], g_ref[...]).astype(BF16)
    a = _dot(hn, win_ref[...])
    q_lat = a[:, :Q_LORA]
    kv_lat = a[:, Q_LORA:Q_LORA + KV_LORA]
    k_pe = a[:, Q_LORA + KV_LORA:Q_LORA + KV_LORA + HEAD_PAD]
    k_pe_rot = a[:, Q_LORA + KV_LORA + HEAD_PAD:]
    cos = c_ref[...]
    sin = s_ref[...]
    k_pe = k_pe * cos + k_pe_rot * sin
    qn = _rms(q_lat, qn_ref[...]).astype(BF16)
    kvn = _rms(kv_lat, kvn_ref[...]).astype(BF16)
    qq = _dot(qn, wq_ref[...])
    kv = _dot(kvn, wkv_ref[...])
    scale = 1.0 / math.sqrt(QK_NOPE + QK_ROPE)
    width = MLA_HEADS * HEAD_PAD
    for h in range(MLA_HEADS):
        lo, hi = h * HEAD_PAD, (h + 1) * HEAD_PAD
        q_h = (qq[:, lo:hi] * cos + qq[:, width + lo:width + hi] * sin) * scale
        q_ref[:, lo:hi] = q_h.astype(BF16)
        k_ref[:, lo:hi] = (kv[:, lo:hi] + k_pe).astype(BF16)
    v_ref[...] = kv[:, width:].astype(BF16)


def _mla_qkv(h, g, win, qn, kvn, wq, wkv, cos, sin):
    t = h.shape[0]
    d = h.shape[1]
    width = MLA_HEADS * HEAD_PAD
    rows = lambda w: pl.BlockSpec((ROW_BLOCK, w), lambda i: (i, 0))
    return pl.pallas_call(
        _mla_qkv_kernel,
        grid=(t // ROW_BLOCK,),
        in_specs=[rows(d), _full(g.shape), _full(win.shape), _full(qn.shape), _full(kvn.shape),
                  _full(wq.shape), _full(wkv.shape), rows(HEAD_PAD), rows(HEAD_PAD)],
        out_specs=[rows(width), rows(width), rows(MLA_HEADS * V_HEAD)],
        out_shape=[jax.ShapeDtypeStruct((t, width), BF16), jax.ShapeDtypeStruct((t, width), BF16),
                   jax.ShapeDtypeStruct((t, MLA_HEADS * V_HEAD), BF16)],
        compiler_params=_params("parallel"),
        name="mla_qkv",
    )(h, g, win, qn, kvn, wq, wkv, cos, sin)


def _attn_kernel(q_ref, k_ref, v_ref, qm_ref, km_ref, vm_ref, o_ref, om_ref, *, blk):
    qi = pl.program_id(1)
    q = q_ref[...]
    km = km_ref[...]
    vm = vm_ref[...]

    s = _dot_nt(q, km)
    m = jnp.max(s, axis=1, keepdims=True)
    p = jnp.exp(s - m)
    l = jnp.sum(p, axis=1, keepdims=True)
    acc = _dot(p.astype(BF16), vm)

    def step(j, carry, diagonal):
        m, l, acc = carry
        off = pl.multiple_of(j * blk, blk)
        s = _dot_nt(q, k_ref[pl.ds(off, blk), :])
        if diagonal:
            row = lax.broadcasted_iota(jnp.int32, (blk, blk), 0)
            col = lax.broadcasted_iota(jnp.int32, (blk, blk), 1)
            s = jnp.where(col <= row, s, -1e30)
        m_new = jnp.maximum(m, jnp.max(s, axis=1, keepdims=True))
        alpha = jnp.exp(m - m_new)
        p = jnp.exp(s - m_new)
        l = alpha * l + jnp.sum(p, axis=1, keepdims=True)
        acc = alpha * acc + _dot(p.astype(BF16), v_ref[pl.ds(off, blk), :])
        return m_new, l, acc

    carry = lax.fori_loop(0, qi, lambda j, c: step(j, c, False), (m, l, acc))
    m, l, acc = step(qi, carry, True)
    o_ref[...] = (acc / l).astype(BF16)

    @pl.when(qi == 0)
    def _meta_queries():
        s = _dot_nt(qm_ref[...], km)
        row = lax.broadcasted_iota(jnp.int32, (N_META, N_META), 0)
        col = lax.broadcasted_iota(jnp.int32, (N_META, N_META), 1)
        s = jnp.where(col <= row, s, -1e30)
        p = jnp.exp(s - jnp.max(s, axis=1, keepdims=True))
        o = _dot(p.astype(BF16), vm) / jnp.sum(p, axis=1, keepdims=True)
        om_ref[...] = o.astype(BF16)


def _attention(q, k, v, nb, seq):
    width = MLA_HEADS * HEAD_PAD
    vwidth = MLA_HEADS * V_HEAD
    steps = seq + N_META
    q2 = q.reshape(steps, nb * width)
    k2 = k.reshape(steps, nb * width)
    v2 = v.reshape(steps, nb * vwidth)
    blk = min(ATTN_BLOCK, seq)
    meta_blk = seq // N_META
    o, om = pl.pallas_call(
        functools.partial(_attn_kernel, blk=blk),
        grid=(nb * MLA_HEADS, seq // blk),
        in_specs=[pl.BlockSpec((blk, HEAD_PAD), lambda bh, qi: (qi, bh)),
                  pl.BlockSpec((seq, HEAD_PAD), lambda bh, qi: (0, bh)),
                  pl.BlockSpec((seq, V_HEAD), lambda bh, qi: (0, bh)),
                  pl.BlockSpec((N_META, HEAD_PAD), lambda bh, qi: (meta_blk, bh)),
                  pl.BlockSpec((N_META, HEAD_PAD), lambda bh, qi: (meta_blk, bh)),
                  pl.BlockSpec((N_META, V_HEAD), lambda bh, qi: (meta_blk, bh))],
        out_specs=[pl.BlockSpec((blk, V_HEAD), lambda bh, qi: (qi, bh)),
                   pl.BlockSpec((N_META, V_HEAD), lambda bh, qi: (0, bh))],
        out_shape=[jax.ShapeDtypeStruct((seq, nb * vwidth), BF16),
                   jax.ShapeDtypeStruct((N_META, nb * vwidth), BF16)],
        compiler_params=_params("parallel", "arbitrary"),
        name="mla_attention",
    )(q2, k2, v2, q2, k2, v2)
    return jnp.concatenate([o.reshape(seq * nb, vwidth), om.reshape(N_META * nb, vwidth)], axis=0)


def _proj_res_kernel(o_ref, w_ref, h_ref, out_ref):
    out_ref[...] = h_ref[...] + _dot(o_ref[...], w_ref[...])


def _proj_res(o, w, h):
    t, d = h.shape
    rows = lambda width: pl.BlockSpec((ROW_BLOCK, width), lambda i: (i, 0))
    return pl.pallas_call(
        _proj_res_kernel,
        grid=(t // ROW_BLOCK,),
        in_specs=[rows(o.shape[1]), _full(w.shape), rows(d)],
        out_specs=rows(d),
        out_shape=jax.ShapeDtypeStruct((t, d), F32),
        compiler_params=_params("parallel"),
        name="mla_out_proj",
    )(o, w, h)


def _lru_kernel(h_ref, g_ref, win_ref, cw_ref, cb_ref, wa_ref, ba_ref, wx_ref, bx_ref, lam_ref,
                wout_ref, out_ref, xbuf, state, *, nb):
    rows, width = h_ref.shape
    hist = (CONV_W - 1) * nb

    @pl.when(pl.program_id(0) == 0)
    def _init():
        xbuf[0:hist, :] = jnp.zeros((hist, width), F32)
        state[...] = jnp.zeros_like(state)

    x = h_ref[...]
    hn = _rms(x, g_ref[...]).astype(BF16)
    proj = _dot(hn, win_ref[...])
    gate = _gelu(proj[:, :width])
    xr = proj[:, width:]
    xbuf[hist:hist + rows, :] = xr
    xc = cb_ref[...] + xr * cw_ref[CONV_W - 1:CONV_W, :]
    for j in range(CONV_W - 1):
        xc = xc + xbuf[j * nb:j * nb + rows, :] * cw_ref[j:j + 1, :]
    xbuf[0:hist, :] = xbuf[rows:rows + hist, :]

    xcb = xc.astype(BF16)
    ra, ix = [], []
    for n in range(LRU_BLOCKS):
        blk = xcb[:, n * LRU_BW:(n + 1) * LRU_BW]
        ra.append(_dot(blk, wa_ref[n]))
        ix.append(_dot(blk, wx_ref[n]))
    r = jax.nn.sigmoid(jnp.concatenate(ra, axis=1) + ba_ref[...])
    i = jax.nn.sigmoid(jnp.concatenate(ix, axis=1) + bx_ref[...])
    lam = lam_ref[...]
    softplus = jnp.maximum(-lam, 0.0) + jnp.log1p(jnp.exp(-jnp.abs(lam)))
    log_a = -LRU_C * r * softplus
    a = jnp.exp(log_a)
    th = jnp.tanh(log_a)
    b = jnp.sqrt(-2.0 * th / (1.0 - th)) * (i * xc)

    hcur = state[...]
    hs = []
    for s in range(rows // nb):
        hcur = a[s * nb:(s + 1) * nb] * hcur + b[s * nb:(s + 1) * nb]
        hs.append(hcur)
    state[...] = hcur
    y = (jnp.concatenate(hs, axis=0) * gate).astype(BF16)
    out_ref[...] = x + _dot(y, wout_ref[...])


def _lru_block(h, g, win, cw, cb, wa, ba, wx, bx, lam, wout, nb):
    t, d = h.shape
    nblk = t // ROW_BLOCK
    rows = pl.BlockSpec((ROW_BLOCK, d), lambda i: ((i + nblk - 1) % nblk, 0))
    args = (g, win, cw, cb, wa, ba, wx, bx, lam, wout)
    return pl.pallas_call(
        functools.partial(_lru_kernel, nb=nb),
        grid=(nblk,),
        in_specs=[rows] + [_full(a.shape) for a in args],
        out_specs=rows,
        out_shape=jax.ShapeDtypeStruct((t, d), F32),
        scratch_shapes=[pltpu.VMEM(((CONV_W - 1) * nb + ROW_BLOCK, d), F32), pltpu.VMEM((nb, d), F32)],
        compiler_params=_params("arbitrary"),
        name="rglru_block",
    )(h, *args)


LANE_TILE = 256
ROW_TILE = 64
RANK_CODE_BASE = 1024.0
RANK_CODE_SCALE = 2.0 ** 90
RANK_NONE = 31.0


def _col_max(tiles):
    while len(tiles) > 1:
        tiles = [jnp.maximum(a, b) for a, b in zip(tiles[0::2], tiles[1::2])] + tiles[len(tiles) & ~1:]
    m = tiles[0]
    for shift in (4, 2, 1):
        m = jnp.maximum(m, pltpu.roll(m, shift, axis=0))
    return m


def _extract_top(tiles, count, emit, rank_codes=False):
    for rank in range(count):
        m = _col_max(tiles)
        emit(rank, m)
        if rank_codes:
            code = -(RANK_CODE_BASE + rank) * RANK_CODE_SCALE
            tiles = [jnp.where(t == m, code, t) for t in tiles]
        elif rank + 1 < count:
            tiles = [jnp.where(t == m, NEG_INF, t) for t in tiles]
    return tiles


def _pair_words(x):
    bits = pltpu.bitcast(x, jnp.uint32)
    return bits | (bits >> 16)


def _peer_kernel(h_ref, g_ref, wqt_ref, sk_ref, u_ref, vt_ref, out_ref,
                 xn_s, s_s, vals_s, rep_s, rk_s, e1_s, n_s, e0_s, rows_s, st_s, wt_s, acc_s):
    c = pl.program_id(1)
    tb = h_ref.shape[0]
    nsub = u_ref.shape[0] // PEER_SUB
    per_sub = PEER_SUB // N_KEYS
    ntop = PEER_TOPK + 1
    vrows = vals_s.shape[1]
    key_tiles = N_KEYS // 8
    lane_tiles = [slice(k * LANE_TILE, (k + 1) * LANE_TILE) for k in range(tb // LANE_TILE)]

    @pl.when(c == 0)
    def _prepare():
        xn_s[...] = _rms(h_ref[...], g_ref[...]).T.astype(BF16)

        def head_tables(h, carry):
            for p in range(2):
                w = wqt_ref[pl.ds(pl.multiple_of((2 * h + p) * N_KEYS, N_KEYS), N_KEYS), :]
                s_s[p] = _dot(sk_ref[p], _dot(w, xn_s[...]).astype(BF16))
            for ls in lane_tiles:
                top0 = {}
                neg = jnp.full((vrows - ntop, LANE_TILE), NEG_INF, F32)

                def emit0(rank, m):
                    top0[rank] = m
                    vals_s[0, rank:rank + 1, ls] = m[0:1]

                def emit1(rank, m):
                    rep_s[rank, :, ls] = m
                    vals_s[1, rank:rank + 1, ls] = m[0:1]

                _extract_top([s_s[0, 8 * k:8 * k + 8, ls] for k in range(key_tiles)], ntop, emit0)
                coded = _extract_top([s_s[1, 8 * k:8 * k + 8, ls] for k in range(key_tiles)], ntop, emit1,
                                     rank_codes=True)
                vals_s[0, ntop:vrows, ls] = neg
                vals_s[1, ntop:vrows, ls] = neg
                v0 = vals_s[0, :, ls]
                v1 = vals_s[1, :, ls]
                cand = ([top0[0] + v1[8 * k:8 * k + 8] for k in range(vrows // 8)]
                        + [top0[a] + v1[0:8] for a in range(1, 8)]
                        + [v0[8 * k:8 * k + 8] + rep_s[0, :, ls] for k in range(1, vrows // 8)])
                got = {}

                def emit(rank, m):
                    got[rank] = m
                _extract_top(cand, ntop, emit)
                z = sum(jnp.exp(got[k] - got[0]) for k in range(PEER_TOPK))
                thr = 0.5 * (got[PEER_TOPK - 1] + got[PEER_TOPK])
                scale = 0.5 / z
                for k in range(0, key_tiles, 2):
                    rows = slice(8 * k, 8 * k + 16)
                    s0 = s_s[0, rows, ls]
                    slack = jnp.concatenate([thr, thr], axis=0) - s0
                    n = jnp.zeros_like(s0)
                    for b in range(ntop):
                        vb = rep_s[b, :, ls]
                        n = jnp.where(jnp.concatenate([vb, vb], axis=0) >= slack, float(b + 1), n)
                    n_s[h, rows, ls] = _pair_words(n)
                    m0 = jnp.concatenate([top0[0], top0[0]], axis=0)
                    e0 = jnp.exp(s0 - m0) * jnp.concatenate([scale, scale], axis=0)
                    e0_s[h, rows, ls] = _pair_words(e0.astype(BF16).astype(F32))
                    m1 = rep_s[0, :, ls]
                    e1 = jnp.exp(s_s[1, rows, ls] - jnp.concatenate([m1, m1], axis=0))
                    e1_s[h, rows, ls] = e1.astype(BF16)
                    t = jnp.concatenate([coded[k], coded[k + 1]], axis=0)
                    rank = jnp.where(t <= -0.5 * RANK_CODE_BASE * RANK_CODE_SCALE,
                                     t * (-1.0 / RANK_CODE_SCALE) - RANK_CODE_BASE, RANK_NONE)
                    rk_s[h, rows, ls] = rank.astype(BF16)
            return carry

        lax.fori_loop(0, PEER_HEADS, head_tables, 0)
        acc_s[...] = jnp.zeros_like(acc_s)

    def scores(sub):
        st_s[sub % 2] = _dot(u_ref[sub * PEER_SUB:(sub + 1) * PEER_SUB, :], xn_s[...])

    def gates(sub):
        buf = sub % 2
        i0 = (c * nsub + sub) * per_sub
        for h in range(PEER_HEADS):
            for ii in range(per_sub):
                rows_s[buf, h, ii, 0] = jnp.broadcast_to(n_s[h, pl.ds(i0 + ii, 1), :], (8, tb))
                rows_s[buf, h, ii, 1] = jnp.broadcast_to(e0_s[h, pl.ds(i0 + ii, 1), :], (8, tb))
        reps = ROW_TILE // 16
        for ls in lane_tiles:
            for jt in range(N_KEYS // ROW_TILE):
                js = slice(jt * ROW_TILE, (jt + 1) * ROW_TILE)
                gate = [jnp.zeros((ROW_TILE, LANE_TILE), BF16) for _ in range(per_sub)]
                for h in range(PEER_HEADS):
                    rk = rk_s[h, js, ls]
                    e1 = e1_s[h, js, ls]
                    for ii in range(per_sub):
                        n = jnp.tile(pltpu.bitcast(rows_s[buf, h, ii, 0, :, ls], BF16), (reps, 1))
                        e = jnp.tile(pltpu.bitcast(rows_s[buf, h, ii, 1, :, ls], BF16), (reps, 1))
                        gate[ii] = gate[ii] + e * jnp.where(rk < n, e1, jnp.zeros_like(e1))
                for ii in range(per_sub):
                    lo = ii * N_KEYS + jt * ROW_TILE
                    s = st_s[buf, lo:lo + ROW_TILE, ls]
                    act = s + s * jnp.tanh(s * (GELU_K0 + GELU_K1 * s * s))
                    wlo = buf * PEER_SUB + lo
                    wt_s[(sub // 2) % 2, wlo:wlo + ROW_TILE, ls] = act.astype(BF16) * gate[ii]

    scores(0)
    for sub in range(nsub):
        if sub + 1 < nsub:
            scores(sub + 1)
        gates(sub)
        if sub % 2 == 1:
            acc_s[...] += _dot(vt_ref[:, (sub - 1) * PEER_SUB:(sub + 1) * PEER_SUB], wt_s[(sub // 2) % 2])

    @pl.when(c == pl.num_programs(1) - 1)
    def _finish():
        out_ref[...] = h_ref[...] + acc_s[...].T


def _peer(h, g, wqt, sk, u, vt):
    t, d = h.shape
    tb = PEER_TOKENS
    n_exp = u.shape[0]
    vrows = 24
    return pl.pallas_call(
        _peer_kernel,
        grid=(t // tb, n_exp // PEER_EXPERTS),
        in_specs=[pl.BlockSpec((tb, d), lambda i, c: (i, 0), pipeline_mode=pl.Buffered(1)),
                  _full(g.shape),
                  pl.BlockSpec(wqt.shape, lambda i, c: (0, 0), pipeline_mode=pl.Buffered(1)),
                  _full(sk.shape),
                  pl.BlockSpec((PEER_EXPERTS, d), lambda i, c: (c, 0)),
                  pl.BlockSpec((d, PEER_EXPERTS), lambda i, c: (0, c))],
        out_specs=pl.BlockSpec((tb, d), lambda i, c: (i, 0)),
        out_shape=jax.ShapeDtypeStruct((t, d), F32),
        scratch_shapes=[pltpu.VMEM((d, tb), BF16),
                        pltpu.VMEM((2, N_KEYS, tb), F32),
                        pltpu.VMEM((2, vrows, tb), F32),
                        pltpu.VMEM((vrows, 8, tb), F32),
                        pltpu.VMEM((PEER_HEADS, N_KEYS, tb), BF16),
                        pltpu.VMEM((PEER_HEADS, N_KEYS, tb), BF16),
                        pltpu.VMEM((PEER_HEADS, N_KEYS, tb), jnp.uint32),
                        pltpu.VMEM((PEER_HEADS, N_KEYS, tb), jnp.uint32),
                        pltpu.VMEM((2, PEER_HEADS, PEER_SUB // N_KEYS, 2, 8, tb), jnp.uint32),
                        pltpu.VMEM((2, PEER_SUB, tb), F32),
                        pltpu.VMEM((2, 2 * PEER_SUB, tb), BF16),
                        pltpu.VMEM((d, tb), F32)],
        compiler_params=_params("parallel", "arbitrary"),
        name="peer_ffn",
    )(h, g, wqt, sk, u, vt)


def _final_norm_kernel(h_ref, g_ref, out_ref):
    out_ref[...] = _rms(h_ref[...], g_ref[...])


def _final_norm(h, g, rows_out):
    d = h.shape[1]
    rows = pl.BlockSpec((ROW_BLOCK, d), lambda i: (i, 0))
    return pl.pallas_call(
        _final_norm_kernel,
        grid=(rows_out // ROW_BLOCK,),
        in_specs=[rows, _full(g.shape)],
        out_specs=rows,
        out_shape=jax.ShapeDtypeStruct((rows_out, d), F32),
        compiler_params=_params("parallel"),
        name="final_norm",
    )(h, g)


def _rot_cols(w):
    half = QK_ROPE // 2
    return jnp.concatenate([-w[:, half:], w[:, :half]], axis=1)


def _pad_rope(w):
    z = jnp.zeros((w.shape[0], QK_NOPE), w.dtype)
    z2 = jnp.zeros((w.shape[0], HEAD_PAD - QK_NOPE - QK_ROPE), w.dtype)
    return jnp.concatenate([z, w, z2], axis=1)


def _mla_weights(w_in, w_q_up, w_kv_up):
    w_kr = w_in[:, Q_LORA + KV_LORA:]
    win = jnp.concatenate([w_in[:, :Q_LORA + KV_LORA], _pad_rope(w_kr), _pad_rope(_rot_cols(w_kr))], axis=1)
    wq = w_q_up.reshape(Q_LORA, MLA_HEADS, QK_NOPE + QK_ROPE)
    pad = jnp.zeros((Q_LORA, MLA_HEADS, HEAD_PAD - QK_NOPE - QK_ROPE), w_q_up.dtype)
    nope0 = jnp.zeros((Q_LORA, MLA_HEADS, QK_NOPE), w_q_up.dtype)
    rope = wq[:, :, QK_NOPE:]
    half = QK_ROPE // 2
    rope_rot = jnp.concatenate([-rope[:, :, half:], rope[:, :, :half]], axis=2)
    q_plain = jnp.concatenate([wq, pad], axis=2).reshape(Q_LORA, MLA_HEADS * HEAD_PAD)
    q_rot = jnp.concatenate([nope0, rope_rot, pad], axis=2).reshape(Q_LORA, MLA_HEADS * HEAD_PAD)
    wkv = w_kv_up.reshape(KV_LORA, MLA_HEADS, QK_NOPE + V_HEAD)
    kpad = jnp.zeros((KV_LORA, MLA_HEADS, HEAD_PAD - QK_NOPE), w_kv_up.dtype)
    k_nope = jnp.concatenate([wkv[:, :, :QK_NOPE], kpad], axis=2).reshape(KV_LORA, MLA_HEADS * HEAD_PAD)
    v = wkv[:, :, QK_NOPE:].reshape(KV_LORA, MLA_HEADS * V_HEAD)
    return (win.astype(BF16), jnp.concatenate([q_plain, q_rot], axis=1).astype(BF16),
            jnp.concatenate([k_nope, v], axis=1).astype(BF16))


def _rope_tables(pos):
    half = QK_ROPE // 2
    freqs = ROPE_THETA ** (-jnp.arange(half, dtype=F32) / half)
    ang = pos.astype(F32)[:, None] * freqs
    n = pos.shape[0]
    ones = jnp.ones((n, QK_NOPE), F32)
    zeros = jnp.zeros((n, QK_NOPE), F32)
    tail = jnp.zeros((n, HEAD_PAD - QK_NOPE - QK_ROPE), F32)
    cos = jnp.concatenate([ones, jnp.cos(ang), jnp.cos(ang), tail], axis=1)
    sin = jnp.concatenate([zeros, jnp.sin(ang), jnp.sin(ang), tail], axis=1)
    return cos, sin


def kernel(x, positions, meta_tokens, mixer_norm, ffn_norm, final_norm, mla_w_in, mla_q_norm, mla_w_q_up, mla_kv_norm, mla_w_kv_up, mla_w_out, lru_w_in, lru_conv_w, lru_conv_b, lru_w_gate_a, lru_b_gate_a, lru_w_gate_x, lru_b_gate_x, lru_lambda, lru_w_out, peer_w_query, peer_sub_keys, peer_u, peer_v):
    nb, seq, d = x.shape
    assert nb % 8 == 0 and (nb * N_META) % ROW_BLOCK == 0 and (nb * seq) % ROW_BLOCK == 0
    assert (nb * (seq + N_META)) % PEER_TOKENS == 0 and seq % min(ATTN_BLOCK, seq) == 0
    row = lambda v: v.reshape(1, -1).astype(F32)

    h = jnp.concatenate([x.transpose(1, 0, 2).reshape(seq * nb, d),
                         jnp.repeat(meta_tokens.astype(x.dtype), nb, axis=0)], axis=0)
    pos = jnp.concatenate([positions.astype(jnp.int32).T.reshape(-1) + N_META,
                           jnp.repeat(jnp.arange(N_META, dtype=jnp.int32), nb)], axis=0)
    cos, sin = _rope_tables(pos)

    def peer(h, layer):
        wqt = peer_w_query[layer].T.astype(BF16)
        return _peer(h, row(ffn_norm[layer]), wqt, peer_sub_keys[layer].astype(BF16),
                     peer_u[layer].astype(BF16), peer_v[layer].astype(BF16).T)

    win, wq, wkv = _mla_weights(mla_w_in[0], mla_w_q_up[0], mla_w_kv_up[0])
    q, k, v = _mla_qkv(h, row(mixer_norm[0]), win, row(mla_q_norm[0]), row(mla_kv_norm[0]), wq, wkv, cos, sin)
    o = _attention(q, k, v, nb, seq)
    h = _proj_res(o, mla_w_out[0].astype(BF16), h)
    h = peer(h, 0)

    h = _lru_block(h, row(mixer_norm[1]), lru_w_in[0].astype(BF16), lru_conv_w[0].astype(F32),
                   row(lru_conv_b[0]), lru_w_gate_a[0].astype(BF16), row(lru_b_gate_a[0]),
                   lru_w_gate_x[0].astype(BF16), row(lru_b_gate_x[0]), row(lru_lambda[0]),
                   lru_w_out[0].astype(BF16), nb)
    h = peer(h, 1)

    out = _final_norm(h, row(final_norm), seq * nb)
    return out.reshape(seq, nb, d).transpose(1, 0, 2)
```

```python
import functools
import math

import jax
import jax.numpy as jnp
from jax import lax
from jax.experimental import pallas as pl
from jax.experimental.pallas import tpu as pltpu

F32 = jnp.float32
BF16 = jnp.bfloat16

N_META = 16
NORM_EPS = 1e-6
MLA_HEADS = 8
QK_NOPE = 128
QK_ROPE = 64
V_HEAD = 128
Q_LORA = 384
KV_LORA = 256
ROPE_THETA = 10000.0
HEAD_PAD = 256
LRU_BLOCKS = 8
LRU_BW = 128
CONV_W = 4
LRU_C = 8.0
PEER_HEADS = 8
N_KEYS = 128
PEER_TOPK = 16
GELU_K0 = math.sqrt(2.0 / math.pi)
GELU_K1 = GELU_K0 * 0.044715
NEG_INF = float("-inf")

ROW_BLOCK = 256
ATTN_BLOCK = 512
PEER_TOKENS = 768
PEER_EXPERTS = 2048
PEER_SUB = 256
VMEM_LIMIT = 58 * 1024 * 1024


def _dot(a, b):
    return jnp.dot(a, b, preferred_element_type=F32)


def _dot_nt(a, b):
    return lax.dot_general(a, b, (((1,), (1,)), ((), ())), preferred_element_type=F32)


def _rms(x, g):
    return x * lax.rsqrt(jnp.mean(x * x, axis=-1, keepdims=True) + NORM_EPS) * g


def _gelu(x):
    return 0.5 * x * (1.0 + jnp.tanh(x * (GELU_K0 + GELU_K1 * x * x)))


def _params(*sem):
    return pltpu.CompilerParams(dimension_semantics=sem, vmem_limit_bytes=VMEM_LIMIT)


def _full(shape):
    return pl.BlockSpec(shape, lambda *_: (0,) * len(shape))


def _mla_qkv_kernel(h_ref, g_ref, win_ref, qn_ref, kvn_ref, wq_ref, wkv_ref, c_ref, s_ref,
                    q_ref, k_ref, v_ref):
    hn = _rms(h_ref[...], g_ref[...]).astype(BF16)
    a = _dot(hn, win_ref[...])
    q_lat = a[:, :Q_LORA]
    kv_lat = a[:, Q_LORA:Q_LORA + KV_LORA]
    k_pe = a[:, Q_LORA + KV_LORA:Q_LORA + KV_LORA + HEAD_PAD]
    k_pe_rot = a[:, Q_LORA + KV_LORA + HEAD_PAD:]
    cos = c_ref[...]
    sin = s_ref[...]
    k_pe = k_pe * cos + k_pe_rot * sin
    qn = _rms(q_lat, qn_ref[...]).astype(BF16)
    kvn = _rms(kv_lat, kvn_ref[...]).astype(BF16)
    qq = _dot(qn, wq_ref[...])
    kv = _dot(kvn, wkv_ref[...])
    scale = 1.0 / math.sqrt(QK_NOPE + QK_ROPE)
    width = MLA_HEADS * HEAD_PAD
    for h in range(MLA_HEADS):
        lo, hi = h * HEAD_PAD, (h + 1) * HEAD_PAD
        q_h = (qq[:, lo:hi] * cos + qq[:, width + lo:width + hi] * sin) * scale
        q_ref[:, lo:hi] = q_h.astype(BF16)
        k_ref[:, lo:hi] = (kv[:, lo:hi] + k_pe).astype(BF16)
    v_ref[...] = kv[:, width:].astype(BF16)


def _mla_qkv(h, g, win, qn, kvn, wq, wkv, cos, sin):
    t = h.shape[0]
    d = h.shape[1]
    width = MLA_HEADS * HEAD_PAD
    rows = lambda w: pl.BlockSpec((ROW_BLOCK, w), lambda i: (i, 0))
    return pl.pallas_call(
        _mla_qkv_kernel,
        grid=(t // ROW_BLOCK,),
        in_specs=[rows(d), _full(g.shape), _full(win.shape), _full(qn.shape), _full(kvn.shape),
                  _full(wq.shape), _full(wkv.shape), rows(HEAD_PAD), rows(HEAD_PAD)],
        out_specs=[rows(width), rows(width), rows(MLA_HEADS * V_HEAD)],
        out_shape=[jax.ShapeDtypeStruct((t, width), BF16), jax.ShapeDtypeStruct((t, width), BF16),
                   jax.ShapeDtypeStruct((t, MLA_HEADS * V_HEAD), BF16)],
        compiler_params=_params("parallel"),
        name="mla_qkv",
    )(h, g, win, qn, kvn, wq, wkv, cos, sin)


def _attn_kernel(q_ref, k_ref, v_ref, qm_ref, km_ref, vm_ref, o_ref, om_ref, *, blk):
    qi = pl.program_id(1)
    q = q_ref[...]
    km = km_ref[...]
    vm = vm_ref[...]

    s = _dot_nt(q, km)
    m = jnp.max(s, axis=1, keepdims=True)
    p = jnp.exp(s - m)
    l = jnp.sum(p, axis=1, keepdims=True)
    acc = _dot(p.astype(BF16), vm)

    def step(j, carry, diagonal):
        m, l, acc = carry
        off = pl.multiple_of(j * blk, blk)
        s = _dot_nt(q, k_ref[pl.ds(off, blk), :])
        if diagonal:
            row = lax.broadcasted_iota(jnp.int32, (blk, blk), 0)
            col = lax.broadcasted_iota(jnp.int32, (blk, blk), 1)
            s = jnp.where(col <= row, s, -1e30)
        m_new = jnp.maximum(m, jnp.max(s, axis=1, keepdims=True))
        alpha = jnp.exp(m - m_new)
        p = jnp.exp(s - m_new)
        l = alpha * l + jnp.sum(p, axis=1, keepdims=True)
        acc = alpha * acc + _dot(p.astype(BF16), v_ref[pl.ds(off, blk), :])
        return m_new, l, acc

    carry = lax.fori_loop(0, qi, lambda j, c: step(j, c, False), (m, l, acc))
    m, l, acc = step(qi, carry, True)
    o_ref[...] = (acc / l).astype(BF16)

    @pl.when(qi == 0)
    def _meta_queries():
        s = _dot_nt(qm_ref[...], km)
        row = lax.broadcasted_iota(jnp.int32, (N_META, N_META), 0)
        col = lax.broadcasted_iota(jnp.int32, (N_META, N_META), 1)
        s = jnp.where(col <= row, s, -1e30)
        p = jnp.exp(s - jnp.max(s, axis=1, keepdims=True))
        o = _dot(p.astype(BF16), vm) / jnp.sum(p, axis=1, keepdims=True)
        om_ref[...] = o.astype(BF16)


def _attention(q, k, v, nb, seq):
    width = MLA_HEADS * HEAD_PAD
    vwidth = MLA_HEADS * V_HEAD
    steps = seq + N_META
    q2 = q.reshape(steps, nb * width)
    k2 = k.reshape(steps, nb * width)
    v2 = v.reshape(steps, nb * vwidth)
    blk = min(ATTN_BLOCK, seq)
    meta_blk = seq // N_META
    o, om = pl.pallas_call(
        functools.partial(_attn_kernel, blk=blk),
        grid=(nb * MLA_HEADS, seq // blk),
        in_specs=[pl.BlockSpec((blk, HEAD_PAD), lambda bh, qi: (qi, bh)),
                  pl.BlockSpec((seq, HEAD_PAD), lambda bh, qi: (0, bh)),
                  pl.BlockSpec((seq, V_HEAD), lambda bh, qi: (0, bh)),
                  pl.BlockSpec((N_META, HEAD_PAD), lambda bh, qi: (meta_blk, bh)),
                  pl.BlockSpec((N_META, HEAD_PAD), lambda bh, qi: (meta_blk, bh)),
                  pl.BlockSpec((N_META, V_HEAD), lambda bh, qi: (meta_blk, bh))],
        out_specs=[pl.BlockSpec((blk, V_HEAD), lambda bh, qi: (qi, bh)),
                   pl.BlockSpec((N_META, V_HEAD), lambda bh, qi: (0, bh))],
        out_shape=[jax.ShapeDtypeStruct((seq, nb * vwidth), BF16),
                   jax.ShapeDtypeStruct((N_META, nb * vwidth), BF16)],
        compiler_params=_params("parallel", "arbitrary"),
        name="mla_attention",
    )(q2, k2, v2, q2, k2, v2)
    return jnp.concatenate([o.reshape(seq * nb, vwidth), om.reshape(N_META * nb, vwidth)], axis=0)


def _proj_res_kernel(o_ref, w_ref, h_ref, out_ref):
    out_ref[...] = h_ref[...] + _dot(o_ref[...], w_ref[...])


def _proj_res(o, w, h):
    t, d = h.shape
    rows = lambda width: pl.BlockSpec((ROW_BLOCK, width), lambda i: (i, 0))
    return pl.pallas_call(
        _proj_res_kernel,
        grid=(t // ROW_BLOCK,),
        in_specs=[rows(o.shape[1]), _full(w.shape), rows(d)],
        out_specs=rows(d),
        out_shape=jax.ShapeDtypeStruct((t, d), F32),
        compiler_params=_params("parallel"),
        name="mla_out_proj",
    )(o, w, h)


def _lru_kernel(h_ref, g_ref, win_ref, cw_ref, cb_ref, wa_ref, ba_ref, wx_ref, bx_ref, lam_ref,
                wout_ref, out_ref, xbuf, state, *, nb):
    rows, width = h_ref.shape
    hist = (CONV_W - 1) * nb

    @pl.when(pl.program_id(0) == 0)
    def _init():
        xbuf[0:hist, :] = jnp.zeros((hist, width), F32)
        state[...] = jnp.zeros_like(state)

    x = h_ref[...]
    hn = _rms(x, g_ref[...]).astype(BF16)
    proj = _dot(hn, win_ref[...])
    gate = _gelu(proj[:, :width])
    xr = proj[:, width:]
    xbuf[hist:hist + rows, :] = xr
    xc = cb_ref[...] + xr * cw_ref[CONV_W - 1:CONV_W, :]
    for j in range(CONV_W - 1):
        xc = xc + xbuf[j * nb:j * nb + rows, :] * cw_ref[j:j + 1, :]
    xbuf[0:hist, :] = xbuf[rows:rows + hist, :]

    xcb = xc.astype(BF16)
    ra, ix = [], []
    for n in range(LRU_BLOCKS):
        blk = xcb[:, n * LRU_BW:(n + 1) * LRU_BW]
        ra.append(_dot(blk, wa_ref[n]))
        ix.append(_dot(blk, wx_ref[n]))
    r = jax.nn.sigmoid(jnp.concatenate(ra, axis=1) + ba_ref[...])
    i = jax.nn.sigmoid(jnp.concatenate(ix, axis=1) + bx_ref[...])
    lam = lam_ref[...]
    softplus = jnp.maximum(-lam, 0.0) + jnp.log1p(jnp.exp(-jnp.abs(lam)))
    log_a = -LRU_C * r * softplus
    a = jnp.exp(log_a)
    th = jnp.tanh(log_a)
    b = jnp.sqrt(-2.0 * th / (1.0 - th)) * (i * xc)

    hcur = state[...]
    hs = []
    for s in range(rows // nb):
        hcur = a[s * nb:(s + 1) * nb] * hcur + b[s * nb:(s + 1) * nb]
        hs.append(hcur)
    state[...] = hcur
    y = (jnp.concatenate(hs, axis=0) * gate).astype(BF16)
    out_ref[...] = x + _dot(y, wout_ref[...])


def _lru_block(h, g, win, cw, cb, wa, ba, wx, bx, lam, wout, nb):
    t, d = h.shape
    nblk = t // ROW_BLOCK
    rows = pl.BlockSpec((ROW_BLOCK, d), lambda i: ((i + nblk - 1) % nblk, 0))
    args = (g, win, cw, cb, wa, ba, wx, bx, lam, wout)
    return pl.pallas_call(
        functools.partial(_lru_kernel, nb=nb),
        grid=(nblk,),
        in_specs=[rows] + [_full(a.shape) for a in args],
        out_specs=rows,
        out_shape=jax.ShapeDtypeStruct((t, d), F32),
        scratch_shapes=[pltpu.VMEM(((CONV_W - 1) * nb + ROW_BLOCK, d), F32), pltpu.VMEM((nb, d), F32)],
        compiler_params=_params("arbitrary"),
        name="rglru_block",
    )(h, *args)


LANE_TILE = 256
ROW_TILE = 64
RANK_CODE_BASE = 1024.0
RANK_CODE_SCALE = 2.0 ** 90
RANK_NONE = 31.0


def _col_max(tiles):
    while len(tiles) > 1:
        tiles = [jnp.maximum(a, b) for a, b in zip(tiles[0::2], tiles[1::2])] + tiles[len(tiles) & ~1:]
    m = tiles[0]
    for shift in (4, 2, 1):
        m = jnp.maximum(m, pltpu.roll(m, shift, axis=0))
    return m


def _extract_top(tiles, count, emit, rank_codes=False):
    for rank in range(count):
        m = _col_max(tiles)
        emit(rank, m)
        if rank_codes:
            code = -(RANK_CODE_BASE + rank) * RANK_CODE_SCALE
            tiles = [jnp.where(t == m, code, t) for t in tiles]
        elif rank + 1 < count:
            tiles = [jnp.where(t == m, NEG_INF, t) for t in tiles]
    return tiles


def _pair_words(x):
    bits = pltpu.bitcast(x, jnp.uint32)
    return bits | (bits >> 16)


def _peer_kernel(h_ref, g_ref, wqt_ref, sk_ref, u_ref, vt_ref, out_ref,
                 xn_s, s_s, vals_s, rep_s, rk_s, e1_s, n_s, e0_s, rows_s, st_s, wt_s, acc_s):
    c = pl.program_id(1)
    tb = h_ref.shape[0]
    nsub = u_ref.shape[0] // PEER_SUB
    per_sub = PEER_SUB // N_KEYS
    ntop = PEER_TOPK + 1
    vrows = vals_s.shape[1]
    key_tiles = N_KEYS // 8
    lane_tiles = [slice(k * LANE_TILE, (k + 1) * LANE_TILE) for k in range(tb // LANE_TILE)]

    @pl.when(c == 0)
    def _prepare():
        xn_s[...] = _rms(h_ref[...], g_ref[...]).T.astype(BF16)

        def head_tables(h, carry):
            for p in range(2):
                w = wqt_ref[pl.ds(pl.multiple_of((2 * h + p) * N_KEYS, N_KEYS), N_KEYS), :]
                s_s[p] = _dot(sk_ref[p], _dot(w, xn_s[...]).astype(BF16))
            for ls in lane_tiles:
                top0 = {}
                neg = jnp.full((vrows - ntop, LANE_TILE), NEG_INF, F32)

                def emit0(rank, m):
                    top0[rank] = m
                    vals_s[0, rank:rank + 1, ls] = m[0:1]

                def emit1(rank, m):
                    rep_s[rank, :, ls] = m
                    vals_s[1, rank:rank + 1, ls] = m[0:1]

                _extract_top([s_s[0, 8 * k:8 * k + 8, ls] for k in range(key_tiles)], ntop, emit0)
                coded = _extract_top([s_s[1, 8 * k:8 * k + 8, ls] for k in range(key_tiles)], ntop, emit1,
                                     rank_codes=True)
                vals_s[0, ntop:vrows, ls] = neg
                vals_s[1, ntop:vrows, ls] = neg
                v0 = vals_s[0, :, ls]
                v1 = vals_s[1, :, ls]
                cand = ([top0[0] + v1[8 * k:8 * k + 8] for k in range(vrows // 8)]
                        + [top0[a] + v1[0:8] for a in range(1, 8)]
                        + [v0[8 * k:8 * k + 8] + rep_s[0, :, ls] for k in range(1, vrows // 8)])
                got = {}

                def emit(rank, m):
                    got[rank] = m
                _extract_top(cand, ntop, emit)
                z = sum(jnp.exp(got[k] - got[0]) for k in range(PEER_TOPK))
                thr = 0.5 * (got[PEER_TOPK - 1] + got[PEER_TOPK])
                scale = 0.5 / z
                for k in range(0, key_tiles, 2):
                    rows = slice(8 * k, 8 * k + 16)
                    s0 = s_s[0, rows, ls]
                    slack = jnp.concatenate([thr, thr], axis=0) - s0
                    n = jnp.zeros_like(s0)
                    for b in range(ntop):
                        vb = rep_s[b, :, ls]
                        n = jnp.where(jnp.concatenate([vb, vb], axis=0) >= slack, float(b + 1), n)
                    n_s[k // 2, h, :, ls] = _pair_words(n)
                    m0 = jnp.concatenate([top0[0], top0[0]], axis=0)
                    e0 = jnp.exp(s0 - m0) * jnp.concatenate([scale, scale], axis=0)
                    e0_s[k // 2, h, :, ls] = _pair_words(e0.astype(BF16).astype(F32))
                    m1 = rep_s[0, :, ls]
                    e1 = jnp.exp(s_s[1, rows, ls] - jnp.concatenate([m1, m1], axis=0))
                    e1_s[h, rows, ls] = e1.astype(BF16)
                    t = jnp.concatenate([coded[k], coded[k + 1]], axis=0)
                    rank = jnp.where(t <= -0.5 * RANK_CODE_BASE * RANK_CODE_SCALE,
                                     t * (-1.0 / RANK_CODE_SCALE) - RANK_CODE_BASE, RANK_NONE)
                    rk_s[h, rows, ls] = rank.astype(BF16)
            return carry

        lax.fori_loop(0, PEER_HEADS, head_tables, 0)
        acc_s[...] = jnp.zeros_like(acc_s)

    def scores(sub):
        st_s[sub % 2] = _dot(u_ref[sub * PEER_SUB:(sub + 1) * PEER_SUB, :], xn_s[...])

    def gates(sub):
        buf = sub % 2
        for h in range(PEER_HEADS):
            for ii in range(per_sub):
                row = sub * per_sub + ii
                rows_s[buf, h, ii, 0] = jnp.broadcast_to(n_s[c, h, row:row + 1, :], (8, tb))
                rows_s[buf, h, ii, 1] = jnp.broadcast_to(e0_s[c, h, row:row + 1, :], (8, tb))
        reps = ROW_TILE // 16
        for ls in lane_tiles:
            for jt in range(N_KEYS // ROW_TILE):
                js = slice(jt * ROW_TILE, (jt + 1) * ROW_TILE)
                gate = [jnp.zeros((ROW_TILE, LANE_TILE), BF16) for _ in range(per_sub)]
                for h in range(PEER_HEADS):
                    rk = rk_s[h, js, ls]
                    e1 = e1_s[h, js, ls]
                    for ii in range(per_sub):
                        n = jnp.tile(pltpu.bitcast(rows_s[buf, h, ii, 0, :, ls], BF16), (reps, 1))
                        e = jnp.tile(pltpu.bitcast(rows_s[buf, h, ii, 1, :, ls], BF16), (reps, 1))
                        gate[ii] = gate[ii] + e * jnp.where(rk < n, e1, jnp.zeros_like(e1))
                for ii in range(per_sub):
                    lo = ii * N_KEYS + jt * ROW_TILE
                    s = st_s[buf, lo:lo + ROW_TILE, ls]
                    act = s + s * jnp.tanh(s * (GELU_K0 + GELU_K1 * s * s))
                    wt_s[buf, lo:lo + ROW_TILE, ls] = act.astype(BF16) * gate[ii]

    scores(0)
    for sub in range(nsub):
        if sub + 1 < nsub:
            scores(sub + 1)
        gates(sub)
        acc_s[...] += _dot(vt_ref[:, sub * PEER_SUB:(sub + 1) * PEER_SUB], wt_s[sub % 2])

    @pl.when(c == pl.num_programs(1) - 1)
    def _finish():
        out_ref[...] = h_ref[...] + acc_s[...].T


def _peer(h, g, wqt, sk, u, vt):
    t, d = h.shape
    tb = PEER_TOKENS
    n_exp = u.shape[0]
    vrows = 24
    keys_per_step = PEER_EXPERTS // N_KEYS
    assert keys_per_step == 16
    step_tables = (N_KEYS // keys_per_step, PEER_HEADS, keys_per_step, tb)
    return pl.pallas_call(
        _peer_kernel,
        grid=(t // tb, n_exp // PEER_EXPERTS),
        in_specs=[pl.BlockSpec((tb, d), lambda i, c: (i, 0), pipeline_mode=pl.Buffered(1)),
                  _full(g.shape),
                  pl.BlockSpec(wqt.shape, lambda i, c: (0, 0), pipeline_mode=pl.Buffered(1)),
                  _full(sk.shape),
                  pl.BlockSpec((PEER_EXPERTS, d), lambda i, c: (c, 0)),
                  pl.BlockSpec((d, PEER_EXPERTS), lambda i, c: (0, c))],
        out_specs=pl.BlockSpec((tb, d), lambda i, c: (i, 0)),
        out_shape=jax.ShapeDtypeStruct((t, d), F32),
        scratch_shapes=[pltpu.VMEM((d, tb), BF16),
                        pltpu.VMEM((2, N_KEYS, tb), F32),
                        pltpu.VMEM((2, vrows, tb), F32),
                        pltpu.VMEM((vrows, 8, tb), F32),
                        pltpu.VMEM((PEER_HEADS, N_KEYS, tb), BF16),
                        pltpu.VMEM((PEER_HEADS, N_KEYS, tb), BF16),
                        pltpu.VMEM(step_tables, jnp.uint32),
                        pltpu.VMEM(step_tables, jnp.uint32),
                        pltpu.VMEM((2, PEER_HEADS, PEER_SUB // N_KEYS, 2, 8, tb), jnp.uint32),
                        pltpu.VMEM((2, PEER_SUB, tb), F32),
                        pltpu.VMEM((2, PEER_SUB, tb), BF16),
                        pltpu.VMEM((d, tb), F32)],
        compiler_params=_params("parallel", "arbitrary"),
        name="peer_ffn",
    )(h, g, wqt, sk, u, vt)


def _final_norm_kernel(h_ref, g_ref, out_ref):
    out_ref[...] = _rms(h_ref[...], g_ref[...])


def _final_norm(h, g, rows_out):
    d = h.shape[1]
    rows = pl.BlockSpec((ROW_BLOCK, d), lambda i: (i, 0))
    return pl.pallas_call(
        _final_norm_kernel,
        grid=(rows_out // ROW_BLOCK,),
        in_specs=[rows, _full(g.shape)],
        out_specs=rows,
        out_shape=jax.ShapeDtypeStruct((rows_out, d), F32),
        compiler_params=_params("parallel"),
        name="final_norm",
    )(h, g)


def _rot_cols(w):
    half = QK_ROPE // 2
    return jnp.concatenate([-w[:, half:], w[:, :half]], axis=1)


def _pad_rope(w):
    z = jnp.zeros((w.shape[0], QK_NOPE), w.dtype)
    z2 = jnp.zeros((w.shape[0], HEAD_PAD - QK_NOPE - QK_ROPE), w.dtype)
    return jnp.concatenate([z, w, z2], axis=1)


def _mla_weights(w_in, w_q_up, w_kv_up):
    w_kr = w_in[:, Q_LORA + KV_LORA:]
    win = jnp.concatenate([w_in[:, :Q_LORA + KV_LORA], _pad_rope(w_kr), _pad_rope(_rot_cols(w_kr))], axis=1)
    wq = w_q_up.reshape(Q_LORA, MLA_HEADS, QK_NOPE + QK_ROPE)
    pad = jnp.zeros((Q_LORA, MLA_HEADS, HEAD_PAD - QK_NOPE - QK_ROPE), w_q_up.dtype)
    nope0 = jnp.zeros((Q_LORA, MLA_HEADS, QK_NOPE), w_q_up.dtype)
    rope = wq[:, :, QK_NOPE:]
    half = QK_ROPE // 2
    rope_rot = jnp.concatenate([-rope[:, :, half:], rope[:, :, :half]], axis=2)
    q_plain = jnp.concatenate([wq, pad], axis=2).reshape(Q_LORA, MLA_HEADS * HEAD_PAD)
    q_rot = jnp.concatenate([nope0, rope_rot, pad], axis=2).reshape(Q_LORA, MLA_HEADS * HEAD_PAD)
    wkv = w_kv_up.reshape(KV_LORA, MLA_HEADS, QK_NOPE + V_HEAD)
    kpad = jnp.zeros((KV_LORA, MLA_HEADS, HEAD_PAD - QK_NOPE), w_kv_up.dtype)
    k_nope = jnp.concatenate([wkv[:, :, :QK_NOPE], kpad], axis=2).reshape(KV_LORA, MLA_HEADS * HEAD_PAD)
    v = wkv[:, :, QK_NOPE:].reshape(KV_LORA, MLA_HEADS * V_HEAD)
    return (win.astype(BF16), jnp.concatenate([q_plain, q_rot], axis=1).astype(BF16),
            jnp.concatenate([k_nope, v], axis=1).astype(BF16))


def _rope_tables(pos):
    half = QK_ROPE // 2
    freqs = ROPE_THETA ** (-jnp.arange(half, dtype=F32) / half)
    ang = pos.astype(F32)[:, None] * freqs
    n = pos.shape[0]
    ones = jnp.ones((n, QK_NOPE), F32)
    zeros = jnp.zeros((n, QK_NOPE), F32)
    tail = jnp.zeros((n, HEAD_PAD - QK_NOPE - QK_ROPE), F32)
    cos = jnp.concatenate([ones, jnp.cos(ang), jnp.cos(ang), tail], axis=1)
    sin = jnp.concatenate([zeros, jnp.sin(ang), jnp.sin(ang), tail], axis=1)
    return cos, sin


def kernel(x, positions, meta_tokens, mixer_norm, ffn_norm, final_norm, mla_w_in, mla_q_norm, mla_w_q_up, mla_kv_norm, mla_w_kv_up, mla_w_out, lru_w_in, lru_conv_w, lru_conv_b, lru_w_gate_a, lru_b_gate_a, lru_w_gate_x, lru_b_gate_x, lru_lambda, lru_w_out, peer_w_query, peer_sub_keys, peer_u, peer_v):
    nb, seq, d = x.shape
    assert nb % 8 == 0 and (nb * N_META) % ROW_BLOCK == 0 and (nb * seq) % ROW_BLOCK == 0
    assert (nb * (seq + N_META)) % PEER_TOKENS == 0 and seq % min(ATTN_BLOCK, seq) == 0
    row = lambda v: v.reshape(1, -1).astype(F32)

    h = jnp.concatenate([x.transpose(1, 0, 2).reshape(seq * nb, d),
                         jnp.repeat(meta_tokens.astype(x.dtype), nb, axis=0)], axis=0)
    pos = jnp.concatenate([positions.astype(jnp.int32).T.reshape(-1) + N_META,
                           jnp.repeat(jnp.arange(N_META, dtype=jnp.int32), nb)], axis=0)
    cos, sin = _rope_tables(pos)

    def peer(h, layer):
        wqt = peer_w_query[layer].T.astype(BF16)
        return _peer(h, row(ffn_norm[layer]), wqt, peer_sub_keys[layer].astype(BF16),
                     peer_u[layer].astype(BF16), peer_v[layer].astype(BF16).T)

    win, wq, wkv = _mla_weights(mla_w_in[0], mla_w_q_up[0], mla_w_kv_up[0])
    q, k, v = _mla_qkv(h, row(mixer_norm[0]), win, row(mla_q_norm[0]), row(mla_kv_norm[0]), wq, wkv, cos, sin)
    o = _attention(q, k, v, nb, seq)
    h = _proj_res(o, mla_w_out[0].astype(BF16), h)
    h = peer(h, 0)

    h = _lru_block(h, row(mixer_norm[1]), lru_w_in[0].astype(BF16), lru_conv_w[0].astype(F32),
                   row(lru_conv_b[0]), lru_w_gate_a[0].astype(BF16), row(lru_b_gate_a[0]),
                   lru_w_gate_x[0].astype(BF16), row(lru_b_gate_x[0]), row(lru_lambda[0]),
                   lru_w_out[0].astype(BF16), nb)
    h = peer(h, 1)

    out = _final_norm(h, row(final_norm), seq * nb)
    return out.reshape(seq, nb, d).transpose(1, 0, 2)
```

```python
import functools
import math

import jax
import jax.numpy as jnp
from jax import lax
from jax.experimental import pallas as pl
from jax.experimental.pallas import tpu as pltpu

F32 = jnp.float32
BF16 = jnp.bfloat16

N_META = 16
NORM_EPS = 1e-6
MLA_HEADS = 8
QK_NOPE = 128
QK_ROPE = 64
V_HEAD = 128
Q_LORA = 384
KV_LORA = 256
ROPE_THETA = 10000.0
HEAD_PAD = 256
LRU_BLOCKS = 8
LRU_BW = 128
CONV_W = 4
LRU_C = 8.0
PEER_HEADS = 8
N_KEYS = 128
PEER_TOPK = 16
GELU_K0 = math.sqrt(2.0 / math.pi)
GELU_K1 = GELU_K0 * 0.044715
NEG_INF = float("-inf")

ROW_BLOCK = 256
ATTN_BLOCK = 512
PEER_TOKENS = 768
PEER_EXPERTS = 2048
PEER_SUB = 256
VMEM_LIMIT = 58 * 1024 * 1024


def _dot(a, b):
    return jnp.dot(a, b, preferred_element_type=F32)


def _dot_nt(a, b):
    return lax.dot_general(a, b, (((1,), (1,)), ((), ())), preferred_element_type=F32)


def _rms(x, g):
    return x * lax.rsqrt(jnp.mean(x * x, axis=-1, keepdims=True) + NORM_EPS) * g


def _gelu(x):
    return 0.5 * x * (1.0 + jnp.tanh(x * (GELU_K0 + GELU_K1 * x * x)))


def _params(*sem):
    return pltpu.CompilerParams(dimension_semantics=sem, vmem_limit_bytes=VMEM_LIMIT)


def _full(shape):
    return pl.BlockSpec(shape, lambda *_: (0,) * len(shape))


def _mla_qkv_kernel(h_ref, g_ref, win_ref, qn_ref, kvn_ref, wq_ref, wkv_ref, c_ref, s_ref,
                    q_ref, k_ref, v_ref):
    hn = _rms(h_ref[...], g_ref[...]).astype(BF16)
    a = _dot(hn, win_ref[...])
    q_lat = a[:, :Q_LORA]
    kv_lat = a[:, Q_LORA:Q_LORA + KV_LORA]
    k_pe = a[:, Q_LORA + KV_LORA:Q_LORA + KV_LORA + HEAD_PAD]
    k_pe_rot = a[:, Q_LORA + KV_LORA + HEAD_PAD:]
    cos = c_ref[...]
    sin = s_ref[...]
    k_pe = k_pe * cos + k_pe_rot * sin
    qn = _rms(q_lat, qn_ref[...]).astype(BF16)
    kvn = _rms(kv_lat, kvn_ref[...]).astype(BF16)
    qq = _dot(qn, wq_ref[...])
    kv = _dot(kvn, wkv_ref[...])
    scale = 1.0 / math.sqrt(QK_NOPE + QK_ROPE)
    width = MLA_HEADS * HEAD_PAD
    for h in range(MLA_HEADS):
        lo, hi = h * HEAD_PAD, (h + 1) * HEAD_PAD
        q_h = (qq[:, lo:hi] * cos + qq[:, width + lo:width + hi] * sin) * scale
        q_ref[:, lo:hi] = q_h.astype(BF16)
        k_ref[:, lo:hi] = (kv[:, lo:hi] + k_pe).astype(BF16)
    v_ref[...] = kv[:, width:].astype(BF16)


def _mla_qkv(h, g, win, qn, kvn, wq, wkv, cos, sin):
    t = h.shape[0]
    d = h.shape[1]
    width = MLA_HEADS * HEAD_PAD
    rows = lambda w: pl.BlockSpec((ROW_BLOCK, w), lambda i: (i, 0))
    return pl.pallas_call(
        _mla_qkv_kernel,
        grid=(t // ROW_BLOCK,),
        in_specs=[rows(d), _full(g.shape), _full(win.shape), _full(qn.shape), _full(kvn.shape),
                  _full(wq.shape), _full(wkv.shape), rows(HEAD_PAD), rows(HEAD_PAD)],
        out_specs=[rows(width), rows(width), rows(MLA_HEADS * V_HEAD)],
        out_shape=[jax.ShapeDtypeStruct((t, width), BF16), jax.ShapeDtypeStruct((t, width), BF16),
                   jax.ShapeDtypeStruct((t, MLA_HEADS * V_HEAD), BF16)],
        compiler_params=_params("parallel"),
        name="mla_qkv",
    )(h, g, win, qn, kvn, wq, wkv, cos, sin)


def _attn_kernel(q_ref, k_ref, v_ref, qm_ref, km_ref, vm_ref, o_ref, om_ref, *, blk):
    qi = pl.program_id(1)
    q = q_ref[...]
    km = km_ref[...]
    vm = vm_ref[...]

    s = _dot_nt(q, km)
    m = jnp.max(s, axis=1, keepdims=True)
    p = jnp.exp(s - m)
    l = jnp.sum(p, axis=1, keepdims=True)
    acc = _dot(p.astype(BF16), vm)

    def step(j, carry, diagonal):
        m, l, acc = carry
        off = pl.multiple_of(j * blk, blk)
        s = _dot_nt(q, k_ref[pl.ds(off, blk), :])
        if diagonal:
            row = lax.broadcasted_iota(jnp.int32, (blk, blk), 0)
            col = lax.broadcasted_iota(jnp.int32, (blk, blk), 1)
            s = jnp.where(col <= row, s, -1e30)
        m_new = jnp.maximum(m, jnp.max(s, axis=1, keepdims=True))
        alpha = jnp.exp(m - m_new)
        p = jnp.exp(s - m_new)
        l = alpha * l + jnp.sum(p, axis=1, keepdims=True)
        acc = alpha * acc + _dot(p.astype(BF16), v_ref[pl.ds(off, blk), :])
        return m_new, l, acc

    carry = lax.fori_loop(0, qi, lambda j, c: step(j, c, False), (m, l, acc))
    m, l, acc = step(qi, carry, True)
    o_ref[...] = (acc / l).astype(BF16)

    @pl.when(qi == 0)
    def _meta_queries():
        s = _dot_nt(qm_ref[...], km)
        row = lax.broadcasted_iota(jnp.int32, (N_META, N_META), 0)
        col = lax.broadcasted_iota(jnp.int32, (N_META, N_META), 1)
        s = jnp.where(col <= row, s, -1e30)
        p = jnp.exp(s - jnp.max(s, axis=1, keepdims=True))
        o = _dot(p.astype(BF16), vm) / jnp.sum(p, axis=1, keepdims=True)
        om_ref[...] = o.astype(BF16)


def _attention(q, k, v, nb, seq):
    width = MLA_HEADS * HEAD_PAD
    vwidth = MLA_HEADS * V_HEAD
    steps = seq + N_META
    q2 = q.reshape(steps, nb * width)
    k2 = k.reshape(steps, nb * width)
    v2 = v.reshape(steps, nb * vwidth)
    blk = min(ATTN_BLOCK, seq)
    meta_blk = seq // N_META
    o, om = pl.pallas_call(
        functools.partial(_attn_kernel, blk=blk),
        grid=(nb * MLA_HEADS, seq // blk),
        in_specs=[pl.BlockSpec((blk, HEAD_PAD), lambda bh, qi: (qi, bh)),
                  pl.BlockSpec((seq, HEAD_PAD), lambda bh, qi: (0, bh)),
                  pl.BlockSpec((seq, V_HEAD), lambda bh, qi: (0, bh)),
                  pl.BlockSpec((N_META, HEAD_PAD), lambda bh, qi: (meta_blk, bh)),
                  pl.BlockSpec((N_META, HEAD_PAD), lambda bh, qi: (meta_blk, bh)),
                  pl.BlockSpec((N_META, V_HEAD), lambda bh, qi: (meta_blk, bh))],
        out_specs=[pl.BlockSpec((blk, V_HEAD), lambda bh, qi: (qi, bh)),
                   pl.BlockSpec((N_META, V_HEAD), lambda bh, qi: (0, bh))],
        out_shape=[jax.ShapeDtypeStruct((seq, nb * vwidth), BF16),
                   jax.ShapeDtypeStruct((N_META, nb * vwidth), BF16)],
        compiler_params=_params("parallel", "arbitrary"),
        name="mla_attention",
    )(q2, k2, v2, q2, k2, v2)
    return jnp.concatenate([o.reshape(seq * nb, vwidth), om.reshape(N_META * nb, vwidth)], axis=0)


def _proj_res_kernel(o_ref, w_ref, h_ref, out_ref):
    out_ref[...] = h_ref[...] + _dot(o_ref[...], w_ref[...])


def _proj_res(o, w, h):
    t, d = h.shape
    rows = lambda width: pl.BlockSpec((ROW_BLOCK, width), lambda i: (i, 0))
    return pl.pallas_call(
        _proj_res_kernel,
        grid=(t // ROW_BLOCK,),
        in_specs=[rows(o.shape[1]), _full(w.shape), rows(d)],
        out_specs=rows(d),
        out_shape=jax.ShapeDtypeStruct((t, d), F32),
        compiler_params=_params("parallel"),
        name="mla_out_proj",
    )(o, w, h)


def _lru_kernel(h_ref, g_ref, win_ref, cw_ref, cb_ref, wa_ref, ba_ref, wx_ref, bx_ref, lam_ref,
                wout_ref, out_ref, xbuf, state, *, nb):
    rows, width = h_ref.shape
    hist = (CONV_W - 1) * nb

    @pl.when(pl.program_id(0) == 0)
    def _init():
        xbuf[0:hist, :] = jnp.zeros((hist, width), F32)
        state[...] = jnp.zeros_like(state)

    x = h_ref[...]
    hn = _rms(x, g_ref[...]).astype(BF16)
    proj = _dot(hn, win_ref[...])
    gate = _gelu(proj[:, :width])
    xr = proj[:, width:]
    xbuf[hist:hist + rows, :] = xr
    xc = cb_ref[...] + xr * cw_ref[CONV_W - 1:CONV_W, :]
    for j in range(CONV_W - 1):
        xc = xc + xbuf[j * nb:j * nb + rows, :] * cw_ref[j:j + 1, :]
    xbuf[0:hist, :] = xbuf[rows:rows + hist, :]

    xcb = xc.astype(BF16)
    ra, ix = [], []
    for n in range(LRU_BLOCKS):
        blk = xcb[:, n * LRU_BW:(n + 1) * LRU_BW]
        ra.append(_dot(blk, wa_ref[n]))
        ix.append(_dot(blk, wx_ref[n]))
    r = jax.nn.sigmoid(jnp.concatenate(ra, axis=1) + ba_ref[...])
    i = jax.nn.sigmoid(jnp.concatenate(ix, axis=1) + bx_ref[...])
    lam = lam_ref[...]
    softplus = jnp.maximum(-lam, 0.0) + jnp.log1p(jnp.exp(-jnp.abs(lam)))
    log_a = -LRU_C * r * softplus
    a = jnp.exp(log_a)
    th = jnp.tanh(log_a)
    b = jnp.sqrt(-2.0 * th / (1.0 - th)) * (i * xc)

    hcur = state[...]
    hs = []
    for s in range(rows // nb):
        hcur = a[s * nb:(s + 1) * nb] * hcur + b[s * nb:(s + 1) * nb]
        hs.append(hcur)
    state[...] = hcur
    y = (jnp.concatenate(hs, axis=0) * gate).astype(BF16)
    out_ref[...] = x + _dot(y, wout_ref[...])


def _lru_block(h, g, win, cw, cb, wa, ba, wx, bx, lam, wout, nb):
    t, d = h.shape
    nblk = t // ROW_BLOCK
    rows = pl.BlockSpec((ROW_BLOCK, d), lambda i: ((i + nblk - 1) % nblk, 0))
    args = (g, win, cw, cb, wa, ba, wx, bx, lam, wout)
    return pl.pallas_call(
        functools.partial(_lru_kernel, nb=nb),
        grid=(nblk,),
        in_specs=[rows] + [_full(a.shape) for a in args],
        out_specs=rows,
        out_shape=jax.ShapeDtypeStruct((t, d), F32),
        scratch_shapes=[pltpu.VMEM(((CONV_W - 1) * nb + ROW_BLOCK, d), F32), pltpu.VMEM((nb, d), F32)],
        compiler_params=_params("arbitrary"),
        name="rglru_block",
    )(h, *args)


LANE_TILE = 256
TOPK_LANES = 128
ROW_TILE = 64
RANK_CODE_BASE = 1024.0
RANK_CODE_SCALE = 2.0 ** 90
RANK_NONE = 31.0


def _col_max(tiles):
    while len(tiles) > 1:
        tiles = [jnp.maximum(a, b) for a, b in zip(tiles[0::2], tiles[1::2])] + tiles[len(tiles) & ~1:]
    m = tiles[0]
    for shift in (4, 2, 1):
        m = jnp.maximum(m, pltpu.roll(m, shift, axis=0))
    return m


def _extract_top(tiles, count, emit, rank_codes=False):
    for rank in range(count):
        m = _col_max(tiles)
        emit(rank, m)
        if rank_codes:
            code = -(RANK_CODE_BASE + rank) * RANK_CODE_SCALE
            tiles = [jnp.where(t == m, code, t) for t in tiles]
        elif rank + 1 < count:
            tiles = [jnp.where(t == m, NEG_INF, t) for t in tiles]
    return tiles


def _pair_words(x):
    bits = pltpu.bitcast(x, jnp.uint32)
    return bits | (bits >> 16)


def _peer_kernel(h_ref, g_ref, wqt_ref, sk_ref, u_ref, vt_ref, og_ref, out_ref,
                 xn_s, s_s, vals_s, rep_s, rk_s, e1_s, n_s, e0_s, rows_s, st_s, wt_s, acc_s, *, norm_output):
    c = pl.program_id(1)
    tb = h_ref.shape[0]
    nsub = u_ref.shape[0] // PEER_SUB
    per_sub = PEER_SUB // N_KEYS
    ntop = PEER_TOPK + 1
    vrows = vals_s.shape[1]
    key_tiles = N_KEYS // 8
    lane_tiles = [slice(k * LANE_TILE, (k + 1) * LANE_TILE) for k in range(tb // LANE_TILE)]

    @pl.when(c == 0)
    def _prepare():
        xn_s[...] = _rms(h_ref[...], g_ref[...]).T.astype(BF16)

        def head_tables(h, carry):
            for p in range(2):
                w = wqt_ref[pl.ds(pl.multiple_of((2 * h + p) * N_KEYS, N_KEYS), N_KEYS), :]
                s_s[p] = _dot(sk_ref[p], _dot(w, xn_s[...]).astype(BF16))
            for ls in [slice(k * TOPK_LANES, (k + 1) * TOPK_LANES) for k in range(tb // TOPK_LANES)]:
                top0 = {}
                neg = jnp.full((vrows - ntop, TOPK_LANES), NEG_INF, F32)

                def emit0(rank, m):
                    top0[rank] = m
                    vals_s[0, rank:rank + 1, ls] = m[0:1]

                def emit1(rank, m):
                    rep_s[rank, :, ls] = m
                    vals_s[1, rank:rank + 1, ls] = m[0:1]

                _extract_top([s_s[0, 8 * k:8 * k + 8, ls] for k in range(key_tiles)], ntop, emit0)
                coded = _extract_top([s_s[1, 8 * k:8 * k + 8, ls] for k in range(key_tiles)], ntop, emit1,
                                     rank_codes=True)
                vals_s[0, ntop:vrows, ls] = neg
                vals_s[1, ntop:vrows, ls] = neg
                v0 = vals_s[0, :, ls]
                v1 = vals_s[1, :, ls]
                cand = ([top0[0] + v1[8 * k:8 * k + 8] for k in range(vrows // 8)]
                        + [top0[a] + v1[0:8] for a in range(1, 8)]
                        + [v0[8 * k:8 * k + 8] + rep_s[0, :, ls] for k in range(1, vrows // 8)])
                got = {}

                def emit(rank, m):
                    got[rank] = m
                _extract_top(cand, ntop, emit)
                z = sum(jnp.exp(got[k] - got[0]) for k in range(PEER_TOPK))
                thr = 0.5 * (got[PEER_TOPK - 1] + got[PEER_TOPK])
                scale = 0.5 / z
                for k in range(0, key_tiles, 2):
                    rows = slice(8 * k, 8 * k + 16)
                    s0 = s_s[0, rows, ls]
                    slack = jnp.concatenate([thr, thr], axis=0) - s0
                    n = jnp.zeros_like(s0)
                    for b in range(ntop):
                        vb = rep_s[b, :, ls]
                        n = jnp.where(jnp.concatenate([vb, vb], axis=0) >= slack, float(b + 1), n)
                    n_s[k // 2, h, :, ls] = _pair_words(n)
                    m0 = jnp.concatenate([top0[0], top0[0]], axis=0)
                    e0 = jnp.exp(s0 - m0) * jnp.concatenate([scale, scale], axis=0)
                    e0_s[k // 2, h, :, ls] = _pair_words(e0.astype(BF16).astype(F32))
                    m1 = rep_s[0, :, ls]
                    e1 = jnp.exp(s_s[1, rows, ls] - jnp.concatenate([m1, m1], axis=0))
                    e1_s[h, rows, ls] = e1.astype(BF16)
                    t = jnp.concatenate([coded[k], coded[k + 1]], axis=0)
                    rank = jnp.where(t <= -0.5 * RANK_CODE_BASE * RANK_CODE_SCALE,
                                     t * (-1.0 / RANK_CODE_SCALE) - RANK_CODE_BASE, RANK_NONE)
                    rk_s[h, rows, ls] = rank.astype(BF16)
            return carry

        lax.fori_loop(0, PEER_HEADS, head_tables, 0)
        acc_s[...] = jnp.zeros_like(acc_s)

    def scores(sub):
        st_s[sub % 2] = _dot(u_ref[sub * PEER_SUB:(sub + 1) * PEER_SUB, :], xn_s[...])

    def gates(sub):
        buf = sub % 2
        for h in range(PEER_HEADS):
            for ii in range(per_sub):
                row = sub * per_sub + ii
                rows_s[buf, h, ii, 0] = jnp.broadcast_to(n_s[c, h, row:row + 1, :], (8, tb))
                rows_s[buf, h, ii, 1] = jnp.broadcast_to(e0_s[c, h, row:row + 1, :], (8, tb))
        reps = ROW_TILE // 16
        for ls in lane_tiles:
            for jt in range(N_KEYS // ROW_TILE):
                js = slice(jt * ROW_TILE, (jt + 1) * ROW_TILE)
                gate = [jnp.zeros((ROW_TILE, LANE_TILE), BF16) for _ in range(per_sub)]
                for h in range(PEER_HEADS):
                    rk = rk_s[h, js, ls]
                    e1 = e1_s[h, js, ls]
                    for ii in range(per_sub):
                        n = jnp.tile(pltpu.bitcast(rows_s[buf, h, ii, 0, :, ls], BF16), (reps, 1))
                        e = jnp.tile(pltpu.bitcast(rows_s[buf, h, ii, 1, :, ls], BF16), (reps, 1))
                        gate[ii] = gate[ii] + e * jnp.where(rk < n, e1, jnp.zeros_like(e1))
                for ii in range(per_sub):
                    lo = ii * N_KEYS + jt * ROW_TILE
                    s = st_s[buf, lo:lo + ROW_TILE, ls]
                    act = s + s * jnp.tanh(s * (GELU_K0 + GELU_K1 * s * s))
                    wt_s[buf, lo:lo + ROW_TILE, ls] = act.astype(BF16) * gate[ii]

    scores(0)
    for sub in range(nsub):
        if sub + 1 < nsub:
            scores(sub + 1)
        gates(sub)
        acc_s[...] += _dot(vt_ref[:, sub * PEER_SUB:(sub + 1) * PEER_SUB], wt_s[sub % 2])

    @pl.when(c == pl.num_programs(1) - 1)
    def _finish():
        out = h_ref[...] + acc_s[...].T
        out_ref[...] = _rms(out, og_ref[...]) if norm_output else out


def _peer(h, g, wqt, sk, u, vt, out_gain, norm_output):
    t, d = h.shape
    tb = PEER_TOKENS
    n_exp = u.shape[0]
    vrows = 24
    keys_per_step = PEER_EXPERTS // N_KEYS
    assert keys_per_step == 16
    step_tables = (N_KEYS // keys_per_step, PEER_HEADS, keys_per_step, tb)
    return pl.pallas_call(
        functools.partial(_peer_kernel, norm_output=norm_output),
        grid=(t // tb, n_exp // PEER_EXPERTS),
        in_specs=[pl.BlockSpec((tb, d), lambda i, c: (i, 0), pipeline_mode=pl.Buffered(1)),
                  _full(g.shape),
                  pl.BlockSpec(wqt.shape, lambda i, c: (0, 0), pipeline_mode=pl.Buffered(1)),
                  _full(sk.shape),
                  pl.BlockSpec((PEER_EXPERTS, d), lambda i, c: (c, 0)),
                  pl.BlockSpec((d, PEER_EXPERTS), lambda i, c: (0, c)),
                  _full(out_gain.shape)],
        out_specs=pl.BlockSpec((tb, d), lambda i, c: (i, 0)),
        out_shape=jax.ShapeDtypeStruct((t, d), F32),
        scratch_shapes=[pltpu.VMEM((d, tb), BF16),
                        pltpu.VMEM((2, N_KEYS, tb), F32),
                        pltpu.VMEM((2, vrows, tb), F32),
                        pltpu.VMEM((vrows, 8, tb), F32),
                        pltpu.VMEM((PEER_HEADS, N_KEYS, tb), BF16),
                        pltpu.VMEM((PEER_HEADS, N_KEYS, tb), BF16),
                        pltpu.VMEM(step_tables, jnp.uint32),
                        pltpu.VMEM(step_tables, jnp.uint32),
                        pltpu.VMEM((2, PEER_HEADS, PEER_SUB // N_KEYS, 2, 8, tb), jnp.uint32),
                        pltpu.VMEM((2, PEER_SUB, tb), F32),
                        pltpu.VMEM((2, PEER_SUB, tb), BF16),
                        pltpu.VMEM((d, tb), F32)],
        compiler_params=_params("parallel", "arbitrary"),
        name="peer_ffn",
    )(h, g, wqt, sk, u, vt, out_gain)


def _rot_cols(w):
    half = QK_ROPE // 2
    return jnp.concatenate([-w[:, half:], w[:, :half]], axis=1)


def _pad_rope(w):
    z = jnp.zeros((w.shape[0], QK_NOPE), w.dtype)
    z2 = jnp.zeros((w.shape[0], HEAD_PAD - QK_NOPE - QK_ROPE), w.dtype)
    return jnp.concatenate([z, w, z2], axis=1)


def _mla_weights(w_in, w_q_up, w_kv_up):
    w_kr = w_in[:, Q_LORA + KV_LORA:]
    win = jnp.concatenate([w_in[:, :Q_LORA + KV_LORA], _pad_rope(w_kr), _pad_rope(_rot_cols(w_kr))], axis=1)
    wq = w_q_up.reshape(Q_LORA, MLA_HEADS, QK_NOPE + QK_ROPE)
    pad = jnp.zeros((Q_LORA, MLA_HEADS, HEAD_PAD - QK_NOPE - QK_ROPE), w_q_up.dtype)
    nope0 = jnp.zeros((Q_LORA, MLA_HEADS, QK_NOPE), w_q_up.dtype)
    rope = wq[:, :, QK_NOPE:]
    half = QK_ROPE // 2
    rope_rot = jnp.concatenate([-rope[:, :, half:], rope[:, :, :half]], axis=2)
    q_plain = jnp.concatenate([wq, pad], axis=2).reshape(Q_LORA, MLA_HEADS * HEAD_PAD)
    q_rot = jnp.concatenate([nope0, rope_rot, pad], axis=2).reshape(Q_LORA, MLA_HEADS * HEAD_PAD)
    wkv = w_kv_up.reshape(KV_LORA, MLA_HEADS, QK_NOPE + V_HEAD)
    kpad = jnp.zeros((KV_LORA, MLA_HEADS, HEAD_PAD - QK_NOPE), w_kv_up.dtype)
    k_nope = jnp.concatenate([wkv[:, :, :QK_NOPE], kpad], axis=2).reshape(KV_LORA, MLA_HEADS * HEAD_PAD)
    v = wkv[:, :, QK_NOPE:].reshape(KV_LORA, MLA_HEADS * V_HEAD)
    return (win.astype(BF16), jnp.concatenate([q_plain, q_rot], axis=1).astype(BF16),
            jnp.concatenate([k_nope, v], axis=1).astype(BF16))


def _rope_tables(pos):
    half = QK_ROPE // 2
    freqs = ROPE_THETA ** (-jnp.arange(half, dtype=F32) / half)
    ang = pos.astype(F32)[:, None] * freqs
    n = pos.shape[0]
    ones = jnp.ones((n, QK_NOPE), F32)
    zeros = jnp.zeros((n, QK_NOPE), F32)
    tail = jnp.zeros((n, HEAD_PAD - QK_NOPE - QK_ROPE), F32)
    cos = jnp.concatenate([ones, jnp.cos(ang), jnp.cos(ang), tail], axis=1)
    sin = jnp.concatenate([zeros, jnp.sin(ang), jnp.sin(ang), tail], axis=1)
    return cos, sin


def kernel(x, positions, meta_tokens, mixer_norm, ffn_norm, final_norm, mla_w_in, mla_q_norm, mla_w_q_up, mla_kv_norm, mla_w_kv_up, mla_w_out, lru_w_in, lru_conv_w, lru_conv_b, lru_w_gate_a, lru_b_gate_a, lru_w_gate_x, lru_b_gate_x, lru_lambda, lru_w_out, peer_w_query, peer_sub_keys, peer_u, peer_v):
    nb, seq, d = x.shape
    assert nb % 8 == 0 and (nb * N_META) % ROW_BLOCK == 0 and (nb * seq) % ROW_BLOCK == 0
    assert (nb * (seq + N_META)) % PEER_TOKENS == 0 and seq % min(ATTN_BLOCK, seq) == 0
    row = lambda v: v.reshape(1, -1).astype(F32)

    h = jnp.concatenate([x.transpose(1, 0, 2).reshape(seq * nb, d),
                         jnp.repeat(meta_tokens.astype(x.dtype), nb, axis=0)], axis=0)
    pos = jnp.concatenate([positions.astype(jnp.int32).T.reshape(-1) + N_META,
                           jnp.repeat(jnp.arange(N_META, dtype=jnp.int32), nb)], axis=0)
    cos, sin = _rope_tables(pos)

    def peer(h, layer, norm_output):
        wqt = peer_w_query[layer].T.astype(BF16)
        return _peer(h, row(ffn_norm[layer]), wqt, peer_sub_keys[layer].astype(BF16),
                     peer_u[layer].astype(BF16), peer_v[layer].astype(BF16).T, row(final_norm), norm_output)

    win, wq, wkv = _mla_weights(mla_w_in[0], mla_w_q_up[0], mla_w_kv_up[0])
    q, k, v = _mla_qkv(h, row(mixer_norm[0]), win, row(mla_q_norm[0]), row(mla_kv_norm[0]), wq, wkv, cos, sin)
    o = _attention(q, k, v, nb, seq)
    h = _proj_res(o, mla_w_out[0].astype(BF16), h)
    h = peer(h, 0, False)

    h = _lru_block(h, row(mixer_norm[1]), lru_w_in[0].astype(BF16), lru_conv_w[0].astype(F32),
                   row(lru_conv_b[0]), lru_w_gate_a[0].astype(BF16), row(lru_b_gate_a[0]),
                   lru_w_gate_x[0].astype(BF16), row(lru_b_gate_x[0]), row(lru_lambda[0]),
                   lru_w_out[0].astype(BF16), nb)
    out = peer(h, 1, True)
    return out[:seq * nb].reshape(seq, nb, d).transpose(1, 0, 2)
```

```python
import functools
import math

import jax
import jax.numpy as jnp
from jax import lax
from jax.experimental import pallas as pl
from jax.experimental.pallas import tpu as pltpu

F32 = jnp.float32
BF16 = jnp.bfloat16

N_META = 16
NORM_EPS = 1e-6
MLA_HEADS = 8
QK_NOPE = 128
QK_ROPE = 64
V_HEAD = 128
Q_LORA = 384
KV_LORA = 256
ROPE_THETA = 10000.0
HEAD_PAD = 256
LRU_BLOCKS = 8
LRU_BW = 128
CONV_W = 4
LRU_C = 8.0
PEER_HEADS = 8
N_KEYS = 128
PEER_TOPK = 16
GELU_K0 = math.sqrt(2.0 / math.pi)
GELU_K1 = GELU_K0 * 0.044715
NEG_INF = float("-inf")

ROW_BLOCK = 256
ATTN_BLOCK = 512
PEER_TOKENS = 768
PEER_EXPERTS = 2048
PEER_SUB = 256
VMEM_LIMIT = 58 * 1024 * 1024


def _dot(a, b):
    return jnp.dot(a, b, preferred_element_type=F32)


def _dot_nt(a, b):
    return lax.dot_general(a, b, (((1,), (1,)), ((), ())), preferred_element_type=F32)


def _rms(x, g):
    return x * lax.rsqrt(jnp.mean(x * x, axis=-1, keepdims=True) + NORM_EPS) * g


def _gelu(x):
    return 0.5 * x * (1.0 + jnp.tanh(x * (GELU_K0 + GELU_K1 * x * x)))


def _params(*sem):
    return pltpu.CompilerParams(dimension_semantics=sem, vmem_limit_bytes=VMEM_LIMIT)


def _full(shape):
    return pl.BlockSpec(shape, lambda *_: (0,) * len(shape))


def _mla_qkv_kernel(h_ref, g_ref, win_ref, qn_ref, kvn_ref, wq_ref, wkv_ref, c_ref, s_ref,
                    q_ref, k_ref, v_ref):
    hn = _rms(h_ref[...], g_ref[...]).astype(BF16)
    a = _dot(hn, win_ref[...])
    q_lat = a[:, :Q_LORA]
    kv_lat = a[:, Q_LORA:Q_LORA + KV_LORA]
    k_pe = a[:, Q_LORA + KV_LORA:Q_LORA + KV_LORA + HEAD_PAD]
    k_pe_rot = a[:, Q_LORA + KV_LORA + HEAD_PAD:]
    cos = c_ref[...]
    sin = s_ref[...]
    k_pe = k_pe * cos + k_pe_rot * sin
    qn = _rms(q_lat, qn_ref[...]).astype(BF16)
    kvn = _rms(kv_lat, kvn_ref[...]).astype(BF16)
    qq = _dot(qn, wq_ref[...])
    kv = _dot(kvn, wkv_ref[...])
    scale = 1.0 / math.sqrt(QK_NOPE + QK_ROPE)
    width = MLA_HEADS * HEAD_PAD
    for h in range(MLA_HEADS):
        lo, hi = h * HEAD_PAD, (h + 1) * HEAD_PAD
        q_h = (qq[:, lo:hi] * cos + qq[:, width + lo:width + hi] * sin) * scale
        q_ref[:, lo:hi] = q_h.astype(BF16)
        k_ref[:, lo:hi] = (kv[:, lo:hi] + k_pe).astype(BF16)
    v_ref[...] = kv[:, width:].astype(BF16)


def _mla_qkv(h, g, win, qn, kvn, wq, wkv, cos, sin):
    t = h.shape[0]
    d = h.shape[1]
    width = MLA_HEADS * HEAD_PAD
    rows = lambda w: pl.BlockSpec((ROW_BLOCK, w), lambda i: (i, 0))
    return pl.pallas_call(
        _mla_qkv_kernel,
        grid=(t // ROW_BLOCK,),
        in_specs=[rows(d), _full(g.shape), _full(win.shape), _full(qn.shape), _full(kvn.shape),
                  _full(wq.shape), _full(wkv.shape), rows(HEAD_PAD), rows(HEAD_PAD)],
        out_specs=[rows(width), rows(width), rows(MLA_HEADS * V_HEAD)],
        out_shape=[jax.ShapeDtypeStruct((t, width), BF16), jax.ShapeDtypeStruct((t, width), BF16),
                   jax.ShapeDtypeStruct((t, MLA_HEADS * V_HEAD), BF16)],
        compiler_params=_params("parallel"),
        name="mla_qkv",
    )(h, g, win, qn, kvn, wq, wkv, cos, sin)


def _attn_kernel(q_ref, k_ref, v_ref, qm_ref, km_ref, vm_ref, o_ref, om_ref, *, blk):
    qi = pl.program_id(1)
    q = q_ref[...]
    km = km_ref[...]
    vm = vm_ref[...]

    s = _dot_nt(q, km)
    m = jnp.max(s, axis=1, keepdims=True)
    p = jnp.exp(s - m)
    l = jnp.sum(p, axis=1, keepdims=True)
    acc = _dot(p.astype(BF16), vm)

    def step(j, carry, diagonal):
        m, l, acc = carry
        off = pl.multiple_of(j * blk, blk)
        s = _dot_nt(q, k_ref[pl.ds(off, blk), :])
        if diagonal:
            row = lax.broadcasted_iota(jnp.int32, (blk, blk), 0)
            col = lax.broadcasted_iota(jnp.int32, (blk, blk), 1)
            s = jnp.where(col <= row, s, -1e30)
        m_new = jnp.maximum(m, jnp.max(s, axis=1, keepdims=True))
        alpha = jnp.exp(m - m_new)
        p = jnp.exp(s - m_new)
        l = alpha * l + jnp.sum(p, axis=1, keepdims=True)
        acc = alpha * acc + _dot(p.astype(BF16), v_ref[pl.ds(off, blk), :])
        return m_new, l, acc

    carry = lax.fori_loop(0, qi, lambda j, c: step(j, c, False), (m, l, acc))
    m, l, acc = step(qi, carry, True)
    o_ref[...] = (acc / l).astype(BF16)

    @pl.when(qi == 0)
    def _meta_queries():
        s = _dot_nt(qm_ref[...], km)
        row = lax.broadcasted_iota(jnp.int32, (N_META, N_META), 0)
        col = lax.broadcasted_iota(jnp.int32, (N_META, N_META), 1)
        s = jnp.where(col <= row, s, -1e30)
        p = jnp.exp(s - jnp.max(s, axis=1, keepdims=True))
        o = _dot(p.astype(BF16), vm) / jnp.sum(p, axis=1, keepdims=True)
        om_ref[...] = o.astype(BF16)


def _attention(q, k, v, nb, seq):
    width = MLA_HEADS * HEAD_PAD
    vwidth = MLA_HEADS * V_HEAD
    steps = seq + N_META
    q2 = q.reshape(steps, nb * width)
    k2 = k.reshape(steps, nb * width)
    v2 = v.reshape(steps, nb * vwidth)
    blk = min(ATTN_BLOCK, seq)
    meta_blk = seq // N_META
    o, om = pl.pallas_call(
        functools.partial(_attn_kernel, blk=blk),
        grid=(nb * MLA_HEADS, seq // blk),
        in_specs=[pl.BlockSpec((blk, HEAD_PAD), lambda bh, qi: (qi, bh)),
                  pl.BlockSpec((seq, HEAD_PAD), lambda bh, qi: (0, bh)),
                  pl.BlockSpec((seq, V_HEAD), lambda bh, qi: (0, bh)),
                  pl.BlockSpec((N_META, HEAD_PAD), lambda bh, qi: (meta_blk, bh)),
                  pl.BlockSpec((N_META, HEAD_PAD), lambda bh, qi: (meta_blk, bh)),
                  pl.BlockSpec((N_META, V_HEAD), lambda bh, qi: (meta_blk, bh))],
        out_specs=[pl.BlockSpec((blk, V_HEAD), lambda bh, qi: (qi, bh)),
                   pl.BlockSpec((N_META, V_HEAD), lambda bh, qi: (0, bh))],
        out_shape=[jax.ShapeDtypeStruct((seq, nb * vwidth), BF16),
                   jax.ShapeDtypeStruct((N_META, nb * vwidth), BF16)],
        compiler_params=_params("parallel", "arbitrary"),
        name="mla_attention",
    )(q2, k2, v2, q2, k2, v2)
    return jnp.concatenate([o.reshape(seq * nb, vwidth), om.reshape(N_META * nb, vwidth)], axis=0)


def _proj_res_kernel(o_ref, w_ref, h_ref, out_ref):
    out_ref[...] = h_ref[...] + _dot(o_ref[...], w_ref[...])


def _proj_res(o, w, h):
    t, d = h.shape
    rows = lambda width: pl.BlockSpec((ROW_BLOCK, width), lambda i: (i, 0))
    return pl.pallas_call(
        _proj_res_kernel,
        grid=(t // ROW_BLOCK,),
        in_specs=[rows(o.shape[1]), _full(w.shape), rows(d)],
        out_specs=rows(d),
        out_shape=jax.ShapeDtypeStruct((t, d), F32),
        compiler_params=_params("parallel"),
        name="mla_out_proj",
    )(o, w, h)


def _lru_kernel(h_ref, g_ref, win_ref, cw_ref, cb_ref, wa_ref, ba_ref, wx_ref, bx_ref, lam_ref,
                wout_ref, out_ref, xbuf, state, *, nb):
    rows, width = h_ref.shape
    hist = (CONV_W - 1) * nb

    @pl.when(pl.program_id(0) == 0)
    def _init():
        xbuf[0:hist, :] = jnp.zeros((hist, width), F32)
        state[...] = jnp.zeros_like(state)

    x = h_ref[...]
    hn = _rms(x, g_ref[...]).astype(BF16)
    proj = _dot(hn, win_ref[...])
    gate = _gelu(proj[:, :width])
    xr = proj[:, width:]
    xbuf[hist:hist + rows, :] = xr
    xc = cb_ref[...] + xr * cw_ref[CONV_W - 1:CONV_W, :]
    for j in range(CONV_W - 1):
        xc = xc + xbuf[j * nb:j * nb + rows, :] * cw_ref[j:j + 1, :]
    xbuf[0:hist, :] = xbuf[rows:rows + hist, :]

    xcb = xc.astype(BF16)
    ra, ix = [], []
    for n in range(LRU_BLOCKS):
        blk = xcb[:, n * LRU_BW:(n + 1) * LRU_BW]
        ra.append(_dot(blk, wa_ref[n]))
        ix.append(_dot(blk, wx_ref[n]))
    r = jax.nn.sigmoid(jnp.concatenate(ra, axis=1) + ba_ref[...])
    i = jax.nn.sigmoid(jnp.concatenate(ix, axis=1) + bx_ref[...])
    lam = lam_ref[...]
    softplus = jnp.maximum(-lam, 0.0) + jnp.log1p(jnp.exp(-jnp.abs(lam)))
    log_a = -LRU_C * r * softplus
    a = jnp.exp(log_a)
    th = jnp.tanh(log_a)
    b = jnp.sqrt(-2.0 * th / (1.0 - th)) * (i * xc)

    hcur = state[...]
    hs = []
    for s in range(rows // nb):
        hcur = a[s * nb:(s + 1) * nb] * hcur + b[s * nb:(s + 1) * nb]
        hs.append(hcur)
    state[...] = hcur
    y = (jnp.concatenate(hs, axis=0) * gate).astype(BF16)
    out_ref[...] = x + _dot(y, wout_ref[...])


def _lru_block(h, g, win, cw, cb, wa, ba, wx, bx, lam, wout, nb):
    t, d = h.shape
    nblk = t // ROW_BLOCK
    rows = pl.BlockSpec((ROW_BLOCK, d), lambda i: ((i + nblk - 1) % nblk, 0))
    args = (g, win, cw, cb, wa, ba, wx, bx, lam, wout)
    return pl.pallas_call(
        functools.partial(_lru_kernel, nb=nb),
        grid=(nblk,),
        in_specs=[rows] + [_full(a.shape) for a in args],
        out_specs=rows,
        out_shape=jax.ShapeDtypeStruct((t, d), F32),
        scratch_shapes=[pltpu.VMEM(((CONV_W - 1) * nb + ROW_BLOCK, d), F32), pltpu.VMEM((nb, d), F32)],
        compiler_params=_params("arbitrary"),
        name="rglru_block",
    )(h, *args)


LANE_TILE = 256
TOPK_LANES = 128
ROW_TILE = 64
RANK_CODE_BASE = 1024.0
RANK_CODE_SCALE = 2.0 ** 90
RANK_NONE = 31.0


def _col_max(tiles):
    while len(tiles) > 1:
        tiles = [jnp.maximum(a, b) for a, b in zip(tiles[0::2], tiles[1::2])] + tiles[len(tiles) & ~1:]
    m = tiles[0]
    for shift in (4, 2, 1):
        m = jnp.maximum(m, pltpu.roll(m, shift, axis=0))
    return m


def _extract_top(tiles, count, emit, rank_codes=False):
    for rank in range(count):
        m = _col_max(tiles)
        emit(rank, m)
        if rank_codes:
            code = -(RANK_CODE_BASE + rank) * RANK_CODE_SCALE
            tiles = [jnp.where(t == m, code, t) for t in tiles]
        elif rank + 1 < count:
            tiles = [jnp.where(t == m, NEG_INF, t) for t in tiles]
    return tiles


def _pair_words(x):
    bits = pltpu.bitcast(x, jnp.uint32)
    return bits | (bits >> 16)


def _peer_kernel(h_ref, g_ref, wqt_ref, sk_ref, u_ref, vt_ref, og_ref, out_ref,
                 xn_s, s_s, vals_s, rep_s, rk_s, e1_s, n_s, e0_s, rows_s, st_s, wt_s, acc_s, *, norm_output):
    c = pl.program_id(1)
    tb = h_ref.shape[0]
    nsub = u_ref.shape[0] // PEER_SUB
    per_sub = PEER_SUB // N_KEYS
    ntop = PEER_TOPK + 1
    vrows = vals_s.shape[1]
    key_tiles = N_KEYS // 8
    lane_tiles = [slice(k * LANE_TILE, (k + 1) * LANE_TILE) for k in range(tb // LANE_TILE)]

    @pl.when(c == 0)
    def _prepare():
        xn_s[...] = _rms(h_ref[...], g_ref[...]).T.astype(BF16)

        def head_scores(h, buf):
            for p in range(2):
                w = wqt_ref[pl.ds(pl.multiple_of((2 * h + p) * N_KEYS, N_KEYS), N_KEYS), :]
                s_s[buf, p] = _dot(sk_ref[p], _dot(w, xn_s[...]).astype(BF16))

        def head_tables(h, buf):
            for ls in [slice(k * TOPK_LANES, (k + 1) * TOPK_LANES) for k in range(tb // TOPK_LANES)]:
                top0 = {}
                neg = jnp.full((vrows - ntop, TOPK_LANES), NEG_INF, F32)

                def emit0(rank, m):
                    top0[rank] = m
                    vals_s[0, rank:rank + 1, ls] = m[0:1]

                def emit1(rank, m):
                    rep_s[rank, :, ls] = m
                    vals_s[1, rank:rank + 1, ls] = m[0:1]

                _extract_top([s_s[buf, 0, 8 * k:8 * k + 8, ls] for k in range(key_tiles)], ntop, emit0)
                coded = _extract_top([s_s[buf, 1, 8 * k:8 * k + 8, ls] for k in range(key_tiles)], ntop, emit1,
                                     rank_codes=True)
                vals_s[0, ntop:vrows, ls] = neg
                vals_s[1, ntop:vrows, ls] = neg
                v0 = vals_s[0, :, ls]
                v1 = vals_s[1, :, ls]
                cand = ([top0[0] + v1[8 * k:8 * k + 8] for k in range(vrows // 8)]
                        + [top0[a] + v1[0:8] for a in range(1, 8)]
                        + [v0[8 * k:8 * k + 8] + rep_s[0, :, ls] for k in range(1, vrows // 8)])
                got = {}

                def emit(rank, m):
                    got[rank] = m
                _extract_top(cand, ntop, emit)
                z = sum(jnp.exp(got[k] - got[0]) for k in range(PEER_TOPK))
                thr = 0.5 * (got[PEER_TOPK - 1] + got[PEER_TOPK])
                scale = 0.5 / z
                for k in range(0, key_tiles, 2):
                    rows = slice(8 * k, 8 * k + 16)
                    s0 = s_s[buf, 0, rows, ls]
                    slack = jnp.concatenate([thr, thr], axis=0) - s0
                    n = jnp.zeros_like(s0)
                    for b in range(ntop):
                        vb = rep_s[b, :, ls]
                        n = jnp.where(jnp.concatenate([vb, vb], axis=0) >= slack, float(b + 1), n)
                    n_s[k // 2, h, :, ls] = _pair_words(n)
                    m0 = jnp.concatenate([top0[0], top0[0]], axis=0)
                    e0 = jnp.exp(s0 - m0) * jnp.concatenate([scale, scale], axis=0)
                    e0_s[k // 2, h, :, ls] = _pair_words(e0.astype(BF16).astype(F32))
                    m1 = rep_s[0, :, ls]
                    e1 = jnp.exp(s_s[buf, 1, rows, ls] - jnp.concatenate([m1, m1], axis=0))
                    e1_s[h, rows, ls] = e1.astype(BF16)
                    t = jnp.concatenate([coded[k], coded[k + 1]], axis=0)
                    rank = jnp.where(t <= -0.5 * RANK_CODE_BASE * RANK_CODE_SCALE,
                                     t * (-1.0 / RANK_CODE_SCALE) - RANK_CODE_BASE, RANK_NONE)
                    rk_s[h, rows, ls] = rank.astype(BF16)

        def head_pair(m, carry):
            h = 2 * m
            head_scores(h + 1, 1)
            head_tables(h, 0)
            head_scores(jnp.minimum(h + 2, PEER_HEADS - 1), 0)
            head_tables(h + 1, 1)
            return carry

        head_scores(0, 0)
        lax.fori_loop(0, PEER_HEADS // 2, head_pair, 0)
        acc_s[...] = jnp.zeros_like(acc_s)

    def scores(sub):
        st_s[sub % 2] = _dot(u_ref[sub * PEER_SUB:(sub + 1) * PEER_SUB, :], xn_s[...])

    def gates(sub):
        buf = sub % 2
        for h in range(PEER_HEADS):
            for ii in range(per_sub):
                row = sub * per_sub + ii
                rows_s[buf, h, ii, 0] = jnp.broadcast_to(n_s[c, h, row:row + 1, :], (8, tb))
                rows_s[buf, h, ii, 1] = jnp.broadcast_to(e0_s[c, h, row:row + 1, :], (8, tb))
        reps = ROW_TILE // 16
        for ls in lane_tiles:
            for jt in range(N_KEYS // ROW_TILE):
                js = slice(jt * ROW_TILE, (jt + 1) * ROW_TILE)
                gate = [jnp.zeros((ROW_TILE, LANE_TILE), BF16) for _ in range(per_sub)]
                for h in range(PEER_HEADS):
                    rk = rk_s[h, js, ls]
                    e1 = e1_s[h, js, ls]
                    for ii in range(per_sub):
                        n = jnp.tile(pltpu.bitcast(rows_s[buf, h, ii, 0, :, ls], BF16), (reps, 1))
                        e = jnp.tile(pltpu.bitcast(rows_s[buf, h, ii, 1, :, ls], BF16), (reps, 1))
                        gate[ii] = gate[ii] + e * jnp.where(rk < n, e1, jnp.zeros_like(e1))
                for ii in range(per_sub):
                    lo = ii * N_KEYS + jt * ROW_TILE
                    s = st_s[buf, lo:lo + ROW_TILE, ls]
                    act = s + s * jnp.tanh(s * (GELU_K0 + GELU_K1 * s * s))
                    wt_s[buf, lo:lo + ROW_TILE, ls] = act.astype(BF16) * gate[ii]

    scores(0)
    for sub in range(nsub):
        if sub + 1 < nsub:
            scores(sub + 1)
        gates(sub)
        acc_s[...] += _dot(vt_ref[:, sub * PEER_SUB:(sub + 1) * PEER_SUB], wt_s[sub % 2])

    @pl.when(c == pl.num_programs(1) - 1)
    def _finish():
        out = h_ref[...] + acc_s[...].T
        out_ref[...] = _rms(out, og_ref[...]) if norm_output else out


def _peer(h, g, wqt, sk, u, vt, out_gain, norm_output):
    t, d = h.shape
    tb = PEER_TOKENS
    n_exp = u.shape[0]
    vrows = 24
    keys_per_step = PEER_EXPERTS // N_KEYS
    assert keys_per_step == 16
    step_tables = (N_KEYS // keys_per_step, PEER_HEADS, keys_per_step, tb)
    return pl.pallas_call(
        functools.partial(_peer_kernel, norm_output=norm_output),
        grid=(t // tb, n_exp // PEER_EXPERTS),
        in_specs=[pl.BlockSpec((tb, d), lambda i, c: (i, 0), pipeline_mode=pl.Buffered(1)),
                  _full(g.shape),
                  pl.BlockSpec(wqt.shape, lambda i, c: (0, 0), pipeline_mode=pl.Buffered(1)),
                  _full(sk.shape),
                  pl.BlockSpec((PEER_EXPERTS, d), lambda i, c: (c, 0)),
                  pl.BlockSpec((d, PEER_EXPERTS), lambda i, c: (0, c)),
                  _full(out_gain.shape)],
        out_specs=pl.BlockSpec((tb, d), lambda i, c: (i, 0)),
        out_shape=jax.ShapeDtypeStruct((t, d), F32),
        scratch_shapes=[pltpu.VMEM((d, tb), BF16),
                        pltpu.VMEM((2, 2, N_KEYS, tb), F32),
                        pltpu.VMEM((2, vrows, tb), F32),
                        pltpu.VMEM((vrows, 8, tb), F32),
                        pltpu.VMEM((PEER_HEADS, N_KEYS, tb), BF16),
                        pltpu.VMEM((PEER_HEADS, N_KEYS, tb), BF16),
                        pltpu.VMEM(step_tables, jnp.uint32),
                        pltpu.VMEM(step_tables, jnp.uint32),
                        pltpu.VMEM((2, PEER_HEADS, PEER_SUB // N_KEYS, 2, 8, tb), jnp.uint32),
                        pltpu.VMEM((2, PEER_SUB, tb), F32),
                        pltpu.VMEM((2, PEER_SUB, tb), BF16),
                        pltpu.VMEM((d, tb), F32)],
        compiler_params=_params("parallel", "arbitrary"),
        name="peer_ffn",
    )(h, g, wqt, sk, u, vt, out_gain)


def _rot_cols(w):
    half = QK_ROPE // 2
    return jnp.concatenate([-w[:, half:], w[:, :half]], axis=1)


def _pad_rope(w):
    z = jnp.zeros((w.shape[0], QK_NOPE), w.dtype)
    z2 = jnp.zeros((w.shape[0], HEAD_PAD - QK_NOPE - QK_ROPE), w.dtype)
    return jnp.concatenate([z, w, z2], axis=1)


def _mla_weights(w_in, w_q_up, w_kv_up):
    w_kr = w_in[:, Q_LORA + KV_LORA:]
    win = jnp.concatenate([w_in[:, :Q_LORA + KV_LORA], _pad_rope(w_kr), _pad_rope(_rot_cols(w_kr))], axis=1)
    wq = w_q_up.reshape(Q_LORA, MLA_HEADS, QK_NOPE + QK_ROPE)
    pad = jnp.zeros((Q_LORA, MLA_HEADS, HEAD_PAD - QK_NOPE - QK_ROPE), w_q_up.dtype)
    nope0 = jnp.zeros((Q_LORA, MLA_HEADS, QK_NOPE), w_q_up.dtype)
    rope = wq[:, :, QK_NOPE:]
    half = QK_ROPE // 2
    rope_rot = jnp.concatenate([-rope[:, :, half:], rope[:, :, :half]], axis=2)
    q_plain = jnp.concatenate([wq, pad], axis=2).reshape(Q_LORA, MLA_HEADS * HEAD_PAD)
    q_rot = jnp.concatenate([nope0, rope_rot, pad], axis=2).reshape(Q_LORA, MLA_HEADS * HEAD_PAD)
    wkv = w_kv_up.reshape(KV_LORA, MLA_HEADS, QK_NOPE + V_HEAD)
    kpad = jnp.zeros((KV_LORA, MLA_HEADS, HEAD_PAD - QK_NOPE), w_kv_up.dtype)
    k_nope = jnp.concatenate([wkv[:, :, :QK_NOPE], kpad], axis=2).reshape(KV_LORA, MLA_HEADS * HEAD_PAD)
    v = wkv[:, :, QK_NOPE:].reshape(KV_LORA, MLA_HEADS * V_HEAD)
    return (win.astype(BF16), jnp.concatenate([q_plain, q_rot], axis=1).astype(BF16),
            jnp.concatenate([k_nope, v], axis=1).astype(BF16))


def _rope_tables(pos):
    half = QK_ROPE // 2
    freqs = ROPE_THETA ** (-jnp.arange(half, dtype=F32) / half)
    ang = pos.astype(F32)[:, None] * freqs
    n = pos.shape[0]
    ones = jnp.ones((n, QK_NOPE), F32)
    zeros = jnp.zeros((n, QK_NOPE), F32)
    tail = jnp.zeros((n, HEAD_PAD - QK_NOPE - QK_ROPE), F32)
    cos = jnp.concatenate([ones, jnp.cos(ang), jnp.cos(ang), tail], axis=1)
    sin = jnp.concatenate([zeros, jnp.sin(ang), jnp.sin(ang), tail], axis=1)
    return cos, sin


def kernel(x, positions, meta_tokens, mixer_norm, ffn_norm, final_norm, mla_w_in, mla_q_norm, mla_w_q_up, mla_kv_norm, mla_w_kv_up, mla_w_out, lru_w_in, lru_conv_w, lru_conv_b, lru_w_gate_a, lru_b_gate_a, lru_w_gate_x, lru_b_gate_x, lru_lambda, lru_w_out, peer_w_query, peer_sub_keys, peer_u, peer_v):
    nb, seq, d = x.shape
    assert nb % 8 == 0 and (nb * N_META) % ROW_BLOCK == 0 and (nb * seq) % ROW_BLOCK == 0
    assert (nb * (seq + N_META)) % PEER_TOKENS == 0 and seq % min(ATTN_BLOCK, seq) == 0
    row = lambda v: v.reshape(1, -1).astype(F32)

    h = jnp.concatenate([x.transpose(1, 0, 2).reshape(seq * nb, d),
                         jnp.repeat(meta_tokens.astype(x.dtype), nb, axis=0)], axis=0)
    pos = jnp.concatenate([positions.astype(jnp.int32).T.reshape(-1) + N_META,
                           jnp.repeat(jnp.arange(N_META, dtype=jnp.int32), nb)], axis=0)
    cos, sin = _rope_tables(pos)

    def peer(h, layer, norm_output):
        wqt = peer_w_query[layer].T.astype(BF16)
        return _peer(h, row(ffn_norm[layer]), wqt, peer_sub_keys[layer].astype(BF16),
                     peer_u[layer].astype(BF16), peer_v[layer].astype(BF16).T, row(final_norm), norm_output)

    win, wq, wkv = _mla_weights(mla_w_in[0], mla_w_q_up[0], mla_w_kv_up[0])
    q, k, v = _mla_qkv(h, row(mixer_norm[0]), win, row(mla_q_norm[0]), row(mla_kv_norm[0]), wq, wkv, cos, sin)
    o = _attention(q, k, v, nb, seq)
    h = _proj_res(o, mla_w_out[0].astype(BF16), h)
    h = peer(h, 0, False)

    h = _lru_block(h, row(mixer_norm[1]), lru_w_in[0].astype(BF16), lru_conv_w[0].astype(F32),
                   row(lru_conv_b[0]), lru_w_gate_a[0].astype(BF16), row(lru_b_gate_a[0]),
                   lru_w_gate_x[0].astype(BF16), row(lru_b_gate_x[0]), row(lru_lambda[0]),
                   lru_w_out[0].astype(BF16), nb)
    out = peer(h, 1, True)
    return out[:seq * nb].reshape(seq, nb, d).transpose(1, 0, 2)
```

```python
import functools
import math

import jax
import jax.numpy as jnp
from jax import lax
from jax.experimental import pallas as pl
from jax.experimental.pallas import tpu as pltpu

F32 = jnp.float32
BF16 = jnp.bfloat16

N_META = 16
NORM_EPS = 1e-6
MLA_HEADS = 8
QK_NOPE = 128
QK_ROPE = 64
V_HEAD = 128
Q_LORA = 384
KV_LORA = 256
ROPE_THETA = 10000.0
HEAD_PAD = 256
LRU_BLOCKS = 8
LRU_BW = 128
CONV_W = 4
LRU_C = 8.0
PEER_HEADS = 8
N_KEYS = 128
PEER_TOPK = 16
GELU_K0 = math.sqrt(2.0 / math.pi)
GELU_K1 = GELU_K0 * 0.044715
NEG_INF = float("-inf")

ROW_BLOCK = 256
ATTN_BLOCK = 512
PEER_TOKENS = 768
PEER_EXPERTS = 2048
PEER_SUB = 256
VMEM_LIMIT = 58 * 1024 * 1024


def _dot(a, b):
    return jnp.dot(a, b, preferred_element_type=F32)


def _dot_nt(a, b):
    return lax.dot_general(a, b, (((1,), (1,)), ((), ())), preferred_element_type=F32)


def _rms(x, g):
    return x * lax.rsqrt(jnp.mean(x * x, axis=-1, keepdims=True) + NORM_EPS) * g


def _gelu(x):
    return 0.5 * x * (1.0 + jnp.tanh(x * (GELU_K0 + GELU_K1 * x * x)))


def _params(*sem):
    return pltpu.CompilerParams(dimension_semantics=sem, vmem_limit_bytes=VMEM_LIMIT)


def _full(shape):
    return pl.BlockSpec(shape, lambda *_: (0,) * len(shape))


def _mla_qkv_kernel(h_ref, g_ref, win_ref, qn_ref, kvn_ref, wq_ref, wkv_ref, c_ref, s_ref,
                    q_ref, k_ref, v_ref):
    hn = _rms(h_ref[...], g_ref[...]).astype(BF16)
    a = _dot(hn, win_ref[...])
    q_lat = a[:, :Q_LORA]
    kv_lat = a[:, Q_LORA:Q_LORA + KV_LORA]
    k_pe = a[:, Q_LORA + KV_LORA:Q_LORA + KV_LORA + HEAD_PAD]
    k_pe_rot = a[:, Q_LORA + KV_LORA + HEAD_PAD:]
    cos = c_ref[...]
    sin = s_ref[...]
    k_pe = k_pe * cos + k_pe_rot * sin
    qn = _rms(q_lat, qn_ref[...]).astype(BF16)
    kvn = _rms(kv_lat, kvn_ref[...]).astype(BF16)
    qq = _dot(qn, wq_ref[...])
    kv = _dot(kvn, wkv_ref[...])
    scale = 1.0 / math.sqrt(QK_NOPE + QK_ROPE)
    width = MLA_HEADS * HEAD_PAD
    for h in range(MLA_HEADS):
        lo, hi = h * HEAD_PAD, (h + 1) * HEAD_PAD
        q_h = (qq[:, lo:hi] * cos + qq[:, width + lo:width + hi] * sin) * scale
        q_ref[:, lo:hi] = q_h.astype(BF16)
        k_ref[:, lo:hi] = (kv[:, lo:hi] + k_pe).astype(BF16)
    v_ref[...] = kv[:, width:].astype(BF16)


def _mla_qkv(h, g, win, qn, kvn, wq, wkv, cos, sin):
    t = h.shape[0]
    d = h.shape[1]
    width = MLA_HEADS * HEAD_PAD
    rows = lambda w: pl.BlockSpec((ROW_BLOCK, w), lambda i: (i, 0))
    return pl.pallas_call(
        _mla_qkv_kernel,
        grid=(t // ROW_BLOCK,),
        in_specs=[rows(d), _full(g.shape), _full(win.shape), _full(qn.shape), _full(kvn.shape),
                  _full(wq.shape), _full(wkv.shape), rows(HEAD_PAD), rows(HEAD_PAD)],
        out_specs=[rows(width), rows(width), rows(MLA_HEADS * V_HEAD)],
        out_shape=[jax.ShapeDtypeStruct((t, width), BF16), jax.ShapeDtypeStruct((t, width), BF16),
                   jax.ShapeDtypeStruct((t, MLA_HEADS * V_HEAD), BF16)],
        compiler_params=_params("parallel"),
        name="mla_qkv",
    )(h, g, win, qn, kvn, wq, wkv, cos, sin)


def _attn_kernel(q_ref, k_ref, v_ref, qm_ref, km_ref, vm_ref, o_ref, om_ref, *, blk):
    qi = pl.program_id(1)
    q = q_ref[...]
    km = km_ref[...]
    vm = vm_ref[...]

    s = _dot_nt(q, km)
    m = jnp.max(s, axis=1, keepdims=True)
    p = jnp.exp(s - m)
    l = jnp.sum(p, axis=1, keepdims=True)
    acc = _dot(p.astype(BF16), vm)

    def step(j, carry, diagonal):
        m, l, acc = carry
        off = pl.multiple_of(j * blk, blk)
        s = _dot_nt(q, k_ref[pl.ds(off, blk), :])
        if diagonal:
            row = lax.broadcasted_iota(jnp.int32, (blk, blk), 0)
            col = lax.broadcasted_iota(jnp.int32, (blk, blk), 1)
            s = jnp.where(col <= row, s, -1e30)
        m_new = jnp.maximum(m, jnp.max(s, axis=1, keepdims=True))
        alpha = jnp.exp(m - m_new)
        p = jnp.exp(s - m_new)
        l = alpha * l + jnp.sum(p, axis=1, keepdims=True)
        acc = alpha * acc + _dot(p.astype(BF16), v_ref[pl.ds(off, blk), :])
        return m_new, l, acc

    carry = lax.fori_loop(0, qi, lambda j, c: step(j, c, False), (m, l, acc))
    m, l, acc = step(qi, carry, True)
    o_ref[...] = (acc / l).astype(BF16)

    @pl.when(qi == 0)
    def _meta_queries():
        s = _dot_nt(qm_ref[...], km)
        row = lax.broadcasted_iota(jnp.int32, (N_META, N_META), 0)
        col = lax.broadcasted_iota(jnp.int32, (N_META, N_META), 1)
        s = jnp.where(col <= row, s, -1e30)
        p = jnp.exp(s - jnp.max(s, axis=1, keepdims=True))
        o = _dot(p.astype(BF16), vm) / jnp.sum(p, axis=1, keepdims=True)
        om_ref[...] = o.astype(BF16)


def _attention(q, k, v, nb, seq):
    width = MLA_HEADS * HEAD_PAD
    vwidth = MLA_HEADS * V_HEAD
    steps = seq + N_META
    q2 = q.reshape(steps, nb * width)
    k2 = k.reshape(steps, nb * width)
    v2 = v.reshape(steps, nb * vwidth)
    blk = min(ATTN_BLOCK, seq)
    meta_blk = seq // N_META
    o, om = pl.pallas_call(
        functools.partial(_attn_kernel, blk=blk),
        grid=(nb * MLA_HEADS, seq // blk),
        in_specs=[pl.BlockSpec((blk, HEAD_PAD), lambda bh, qi: (qi, bh)),
                  pl.BlockSpec((seq, HEAD_PAD), lambda bh, qi: (0, bh)),
                  pl.BlockSpec((seq, V_HEAD), lambda bh, qi: (0, bh)),
                  pl.BlockSpec((N_META, HEAD_PAD), lambda bh, qi: (meta_blk, bh)),
                  pl.BlockSpec((N_META, HEAD_PAD), lambda bh, qi: (meta_blk, bh)),
                  pl.BlockSpec((N_META, V_HEAD), lambda bh, qi: (meta_blk, bh))],
        out_specs=[pl.BlockSpec((blk, V_HEAD), lambda bh, qi: (qi, bh)),
                   pl.BlockSpec((N_META, V_HEAD), lambda bh, qi: (0, bh))],
        out_shape=[jax.ShapeDtypeStruct((seq, nb * vwidth), BF16),
                   jax.ShapeDtypeStruct((N_META, nb * vwidth), BF16)],
        compiler_params=_params("parallel", "arbitrary"),
        name="mla_attention",
    )(q2, k2, v2, q2, k2, v2)
    return jnp.concatenate([o.reshape(seq * nb, vwidth), om.reshape(N_META * nb, vwidth)], axis=0)


def _proj_res_kernel(o_ref, w_ref, h_ref, out_ref):
    out_ref[...] = h_ref[...] + _dot(o_ref[...], w_ref[...])


def _proj_res(o, w, h):
    t, d = h.shape
    rows = lambda width: pl.BlockSpec((ROW_BLOCK, width), lambda i: (i, 0))
    return pl.pallas_call(
        _proj_res_kernel,
        grid=(t // ROW_BLOCK,),
        in_specs=[rows(o.shape[1]), _full(w.shape), rows(d)],
        out_specs=rows(d),
        out_shape=jax.ShapeDtypeStruct((t, d), F32),
        compiler_params=_params("parallel"),
        name="mla_out_proj",
    )(o, w, h)


def _lru_kernel(h_ref, g_ref, win_ref, cw_ref, cb_ref, wa_ref, ba_ref, wx_ref, bx_ref, lam_ref,
                wout_ref, out_ref, xbuf, state, *, nb):
    rows, width = h_ref.shape
    hist = (CONV_W - 1) * nb

    @pl.when(pl.program_id(0) == 0)
    def _init():
        xbuf[0:hist, :] = jnp.zeros((hist, width), F32)
        state[...] = jnp.zeros_like(state)

    x = h_ref[...]
    hn = _rms(x, g_ref[...]).astype(BF16)
    proj = _dot(hn, win_ref[...])
    gate = _gelu(proj[:, :width])
    xr = proj[:, width:]
    xbuf[hist:hist + rows, :] = xr
    xc = cb_ref[...] + xr * cw_ref[CONV_W - 1:CONV_W, :]
    for j in range(CONV_W - 1):
        xc = xc + xbuf[j * nb:j * nb + rows, :] * cw_ref[j:j + 1, :]
    xbuf[0:hist, :] = xbuf[rows:rows + hist, :]

    xcb = xc.astype(BF16)
    ra, ix = [], []
    for n in range(LRU_BLOCKS):
        blk = xcb[:, n * LRU_BW:(n + 1) * LRU_BW]
        ra.append(_dot(blk, wa_ref[n]))
        ix.append(_dot(blk, wx_ref[n]))
    r = jax.nn.sigmoid(jnp.concatenate(ra, axis=1) + ba_ref[...])
    i = jax.nn.sigmoid(jnp.concatenate(ix, axis=1) + bx_ref[...])
    lam = lam_ref[...]
    softplus = jnp.maximum(-lam, 0.0) + jnp.log1p(jnp.exp(-jnp.abs(lam)))
    log_a = -LRU_C * r * softplus
    a = jnp.exp(log_a)
    th = jnp.tanh(log_a)
    b = jnp.sqrt(-2.0 * th / (1.0 - th)) * (i * xc)

    hcur = state[...]
    hs = []
    for s in range(rows // nb):
        hcur = a[s * nb:(s + 1) * nb] * hcur + b[s * nb:(s + 1) * nb]
        hs.append(hcur)
    state[...] = hcur
    y = (jnp.concatenate(hs, axis=0) * gate).astype(BF16)
    out_ref[...] = x + _dot(y, wout_ref[...])


def _lru_block(h, g, win, cw, cb, wa, ba, wx, bx, lam, wout, nb):
    t, d = h.shape
    nblk = t // ROW_BLOCK
    rows = pl.BlockSpec((ROW_BLOCK, d), lambda i: ((i + nblk - 1) % nblk, 0))
    args = (g, win, cw, cb, wa, ba, wx, bx, lam, wout)
    return pl.pallas_call(
        functools.partial(_lru_kernel, nb=nb),
        grid=(nblk,),
        in_specs=[rows] + [_full(a.shape) for a in args],
        out_specs=rows,
        out_shape=jax.ShapeDtypeStruct((t, d), F32),
        scratch_shapes=[pltpu.VMEM(((CONV_W - 1) * nb + ROW_BLOCK, d), F32), pltpu.VMEM((nb, d), F32)],
        compiler_params=_params("arbitrary"),
        name="rglru_block",
    )(h, *args)


LANE_TILE = 256
TOPK_LANES = 128
ROW_TILE = 32
RANK_CODE_BASE = 1024.0
RANK_CODE_SCALE = 2.0 ** 90
RANK_NONE = 31.0


def _col_max(tiles):
    while len(tiles) > 1:
        tiles = [jnp.maximum(a, b) for a, b in zip(tiles[0::2], tiles[1::2])] + tiles[len(tiles) & ~1:]
    m = tiles[0]
    for shift in (4, 2, 1):
        m = jnp.maximum(m, pltpu.roll(m, shift, axis=0))
    return m


def _extract_top(tiles, count, emit, rank_codes=False):
    for rank in range(count):
        m = _col_max(tiles)
        emit(rank, m)
        if rank_codes:
            code = -(RANK_CODE_BASE + rank) * RANK_CODE_SCALE
            tiles = [jnp.where(t == m, code, t) for t in tiles]
        elif rank + 1 < count:
            tiles = [jnp.where(t == m, NEG_INF, t) for t in tiles]
    return tiles


def _pair_words(x):
    bits = pltpu.bitcast(x, jnp.uint32)
    return bits | (bits >> 16)


def _peer_kernel(h_ref, g_ref, wqt_ref, sk_ref, u_ref, vt_ref, og_ref, out_ref,
                 xn_s, s_s, vals_s, rep_s, rk_s, e1_s, n_s, e0_s, rows_s, st_s, wt_s, acc_s, *, norm_output):
    c = pl.program_id(1)
    tb = h_ref.shape[0]
    nsub = u_ref.shape[0] // PEER_SUB
    per_sub = PEER_SUB // N_KEYS
    ntop = PEER_TOPK + 1
    vrows = vals_s.shape[1]
    key_tiles = N_KEYS // 8
    lane_tiles = [slice(k * LANE_TILE, (k + 1) * LANE_TILE) for k in range(tb // LANE_TILE)]

    @pl.when(c == 0)
    def _prepare():
        xn_s[...] = _rms(h_ref[...], g_ref[...]).T.astype(BF16)

        def head_scores(h, buf):
            for p in range(2):
                w = wqt_ref[pl.ds(pl.multiple_of((2 * h + p) * N_KEYS, N_KEYS), N_KEYS), :]
                s_s[buf, p] = _dot(sk_ref[p], _dot(w, xn_s[...]).astype(BF16))

        def head_tables(h, buf):
            for ls in [slice(k * TOPK_LANES, (k + 1) * TOPK_LANES) for k in range(tb // TOPK_LANES)]:
                top0 = {}
                neg = jnp.full((vrows - ntop, TOPK_LANES), NEG_INF, F32)

                def emit0(rank, m):
                    top0[rank] = m
                    vals_s[0, rank:rank + 1, ls] = m[0:1]

                def emit1(rank, m):
                    rep_s[rank, :, ls] = m
                    vals_s[1, rank:rank + 1, ls] = m[0:1]

                _extract_top([s_s[buf, 0, 8 * k:8 * k + 8, ls] for k in range(key_tiles)], ntop, emit0)
                coded = _extract_top([s_s[buf, 1, 8 * k:8 * k + 8, ls] for k in range(key_tiles)], ntop, emit1,
                                     rank_codes=True)
                vals_s[0, ntop:vrows, ls] = neg
                vals_s[1, ntop:vrows, ls] = neg
                v0 = vals_s[0, :, ls]
                v1 = vals_s[1, :, ls]
                cand = ([top0[0] + v1[8 * k:8 * k + 8] for k in range(vrows // 8)]
                        + [top0[a] + v1[0:8] for a in range(1, 8)]
                        + [v0[8 * k:8 * k + 8] + rep_s[0, :, ls] for k in range(1, vrows // 8)])
                got = {}

                def emit(rank, m):
                    got[rank] = m
                _extract_top(cand, ntop, emit)
                z = sum(jnp.exp(got[k] - got[0]) for k in range(PEER_TOPK))
                thr = 0.5 * (got[PEER_TOPK - 1] + got[PEER_TOPK])
                scale = 0.5 / z
                for k in range(0, key_tiles, 2):
                    rows = slice(8 * k, 8 * k + 16)
                    s0 = s_s[buf, 0, rows, ls]
                    slack = jnp.concatenate([thr, thr], axis=0) - s0
                    n = jnp.zeros_like(s0)
                    for b in range(ntop):
                        vb = rep_s[b, :, ls]
                        n = jnp.where(jnp.concatenate([vb, vb], axis=0) >= slack, float(b + 1), n)
                    n_s[k // 2, h, :, ls] = _pair_words(n)
                    m0 = jnp.concatenate([top0[0], top0[0]], axis=0)
                    e0 = jnp.exp(s0 - m0) * jnp.concatenate([scale, scale], axis=0)
                    e0_s[k // 2, h, :, ls] = _pair_words(e0.astype(BF16).astype(F32))
                    m1 = rep_s[0, :, ls]
                    e1 = jnp.exp(s_s[buf, 1, rows, ls] - jnp.concatenate([m1, m1], axis=0))
                    e1_s[h, rows, ls] = e1.astype(BF16)
                    t = jnp.concatenate([coded[k], coded[k + 1]], axis=0)
                    rank = jnp.where(t <= -0.5 * RANK_CODE_BASE * RANK_CODE_SCALE,
                                     t * (-1.0 / RANK_CODE_SCALE) - RANK_CODE_BASE, RANK_NONE)
                    rk_s[h, rows, ls] = rank.astype(BF16)

        def head_pair(m, carry):
            h = 2 * m
            head_scores(h + 1, 1)
            head_tables(h, 0)
            head_scores(jnp.minimum(h + 2, PEER_HEADS - 1), 0)
            head_tables(h + 1, 1)
            return carry

        head_scores(0, 0)
        lax.fori_loop(0, PEER_HEADS // 2, head_pair, 0)
        acc_s[...] = jnp.zeros_like(acc_s)

    def scores(sub):
        st_s[sub % 2] = _dot(u_ref[sub * PEER_SUB:(sub + 1) * PEER_SUB, :], xn_s[...])

    def gates(sub):
        buf = sub % 2
        for h in range(PEER_HEADS):
            for ii in range(per_sub):
                row = sub * per_sub + ii
                rows_s[buf, h, ii, 0] = jnp.broadcast_to(n_s[c, h, row:row + 1, :], (8, tb))
                rows_s[buf, h, ii, 1] = jnp.broadcast_to(e0_s[c, h, row:row + 1, :], (8, tb))
        reps = ROW_TILE // 16
        for ls in lane_tiles:
            for jt in range(N_KEYS // ROW_TILE):
                js = slice(jt * ROW_TILE, (jt + 1) * ROW_TILE)
                gate = [jnp.zeros((ROW_TILE, LANE_TILE), BF16) for _ in range(per_sub)]
                for h in range(PEER_HEADS):
                    rk = rk_s[h, js, ls]
                    e1 = e1_s[h, js, ls]
                    for ii in range(per_sub):
                        n = jnp.tile(pltpu.bitcast(rows_s[buf, h, ii, 0, :, ls], BF16), (reps, 1))
                        e = jnp.tile(pltpu.bitcast(rows_s[buf, h, ii, 1, :, ls], BF16), (reps, 1))
                        gate[ii] = gate[ii] + e * jnp.where(rk < n, e1, jnp.zeros_like(e1))
                for ii in range(per_sub):
                    lo = ii * N_KEYS + jt * ROW_TILE
                    s = st_s[buf, lo:lo + ROW_TILE, ls]
                    act = s + s * jnp.tanh(s * (GELU_K0 + GELU_K1 * s * s))
                    wt_s[buf, lo:lo + ROW_TILE, ls] = act.astype(BF16) * gate[ii]

    scores(0)
    for sub in range(nsub):
        if sub + 1 < nsub:
            scores(sub + 1)
        gates(sub)
        acc_s[...] += _dot(vt_ref[:, sub * PEER_SUB:(sub + 1) * PEER_SUB], wt_s[sub % 2])

    @pl.when(c == pl.num_programs(1) - 1)
    def _finish():
        out = h_ref[...] + acc_s[...].T
        out_ref[...] = _rms(out, og_ref[...]) if norm_output else out


def _peer(h, g, wqt, sk, u, vt, out_gain, norm_output):
    t, d = h.shape
    tb = PEER_TOKENS
    n_exp = u.shape[0]
    vrows = 24
    keys_per_step = PEER_EXPERTS // N_KEYS
    assert keys_per_step == 16
    step_tables = (N_KEYS // keys_per_step, PEER_HEADS, keys_per_step, tb)
    return pl.pallas_call(
        functools.partial(_peer_kernel, norm_output=norm_output),
        grid=(t // tb, n_exp // PEER_EXPERTS),
        in_specs=[pl.BlockSpec((tb, d), lambda i, c: (i, 0), pipeline_mode=pl.Buffered(1)),
                  _full(g.shape),
                  pl.BlockSpec(wqt.shape, lambda i, c: (0, 0), pipeline_mode=pl.Buffered(1)),
                  _full(sk.shape),
                  pl.BlockSpec((PEER_EXPERTS, d), lambda i, c: (c, 0)),
                  pl.BlockSpec((d, PEER_EXPERTS), lambda i, c: (0, c)),
                  _full(out_gain.shape)],
        out_specs=pl.BlockSpec((tb, d), lambda i, c: (i, 0)),
        out_shape=jax.ShapeDtypeStruct((t, d), F32),
        scratch_shapes=[pltpu.VMEM((d, tb), BF16),
                        pltpu.VMEM((2, 2, N_KEYS, tb), F32),
                        pltpu.VMEM((2, vrows, tb), F32),
                        pltpu.VMEM((vrows, 8, tb), F32),
                        pltpu.VMEM((PEER_HEADS, N_KEYS, tb), BF16),
                        pltpu.VMEM((PEER_HEADS, N_KEYS, tb), BF16),
                        pltpu.VMEM(step_tables, jnp.uint32),
                        pltpu.VMEM(step_tables, jnp.uint32),
                        pltpu.VMEM((2, PEER_HEADS, PEER_SUB // N_KEYS, 2, 8, tb), jnp.uint32),
                        pltpu.VMEM((2, PEER_SUB, tb), F32),
                        pltpu.VMEM((2, PEER_SUB, tb), BF16),
                        pltpu.VMEM((d, tb), F32)],
        compiler_params=_params("parallel", "arbitrary"),
        name="peer_ffn",
    )(h, g, wqt, sk, u, vt, out_gain)


def _rot_cols(w):
    half = QK_ROPE // 2
    return jnp.concatenate([-w[:, half:], w[:, :half]], axis=1)


def _pad_rope(w):
    z = jnp.zeros((w.shape[0], QK_NOPE), w.dtype)
    z2 = jnp.zeros((w.shape[0], HEAD_PAD - QK_NOPE - QK_ROPE), w.dtype)
    return jnp.concatenate([z, w, z2], axis=1)


def _mla_weights(w_in, w_q_up, w_kv_up):
    w_kr = w_in[:, Q_LORA + KV_LORA:]
    win = jnp.concatenate([w_in[:, :Q_LORA + KV_LORA], _pad_rope(w_kr), _pad_rope(_rot_cols(w_kr))], axis=1)
    wq = w_q_up.reshape(Q_LORA, MLA_HEADS, QK_NOPE + QK_ROPE)
    pad = jnp.zeros((Q_LORA, MLA_HEADS, HEAD_PAD - QK_NOPE - QK_ROPE), w_q_up.dtype)
    nope0 = jnp.zeros((Q_LORA, MLA_HEADS, QK_NOPE), w_q_up.dtype)
    rope = wq[:, :, QK_NOPE:]
    half = QK_ROPE // 2
    rope_rot = jnp.concatenate([-rope[:, :, half:], rope[:, :, :half]], axis=2)
    q_plain = jnp.concatenate([wq, pad], axis=2).reshape(Q_LORA, MLA_HEADS * HEAD_PAD)
    q_rot = jnp.concatenate([nope0, rope_rot, pad], axis=2).reshape(Q_LORA, MLA_HEADS * HEAD_PAD)
    wkv = w_kv_up.reshape(KV_LORA, MLA_HEADS, QK_NOPE + V_HEAD)
    kpad = jnp.zeros((KV_LORA, MLA_HEADS, HEAD_PAD - QK_NOPE), w_kv_up.dtype)
    k_nope = jnp.concatenate([wkv[:, :, :QK_NOPE], kpad], axis=2).reshape(KV_LORA, MLA_HEADS * HEAD_PAD)
    v = wkv[:, :, QK_NOPE:].reshape(KV_LORA, MLA_HEADS * V_HEAD)
    return (win.astype(BF16), jnp.concatenate([q_plain, q_rot], axis=1).astype(BF16),
            jnp.concatenate([k_nope, v], axis=1).astype(BF16))


def _rope_tables(pos):
    half = QK_ROPE // 2
    freqs = ROPE_THETA ** (-jnp.arange(half, dtype=F32) / half)
    ang = pos.astype(F32)[:, None] * freqs
    n = pos.shape[0]
    ones = jnp.ones((n, QK_NOPE), F32)
    zeros = jnp.zeros((n, QK_NOPE), F32)
    tail = jnp.zeros((n, HEAD_PAD - QK_NOPE - QK_ROPE), F32)
    cos = jnp.concatenate([ones, jnp.cos(ang), jnp.cos(ang), tail], axis=1)
    sin = jnp.concatenate([zeros, jnp.sin(ang), jnp.sin(ang), tail], axis=1)
    return cos, sin


def kernel(x, positions, meta_tokens, mixer_norm, ffn_norm, final_norm, mla_w_in, mla_q_norm, mla_w_q_up, mla_kv_norm, mla_w_kv_up, mla_w_out, lru_w_in, lru_conv_w, lru_conv_b, lru_w_gate_a, lru_b_gate_a, lru_w_gate_x, lru_b_gate_x, lru_lambda, lru_w_out, peer_w_query, peer_sub_keys, peer_u, peer_v):
    nb, seq, d = x.shape
    assert nb % 8 == 0 and (nb * N_META) % ROW_BLOCK == 0 and (nb * seq) % ROW_BLOCK == 0
    assert (nb * (seq + N_META)) % PEER_TOKENS == 0 and seq % min(ATTN_BLOCK, seq) == 0
    row = lambda v: v.reshape(1, -1).astype(F32)

    h = jnp.concatenate([x.transpose(1, 0, 2).reshape(seq * nb, d),
                         jnp.repeat(meta_tokens.astype(x.dtype), nb, axis=0)], axis=0)
    pos = jnp.concatenate([positions.astype(jnp.int32).T.reshape(-1) + N_META,
                           jnp.repeat(jnp.arange(N_META, dtype=jnp.int32), nb)], axis=0)
    cos, sin = _rope_tables(pos)

    def peer(h, layer, norm_output):
        wqt = peer_w_query[layer].T.astype(BF16)
        return _peer(h, row(ffn_norm[layer]), wqt, peer_sub_keys[layer].astype(BF16),
                     peer_u[layer].astype(BF16), peer_v[layer].astype(BF16).T, row(final_norm), norm_output)

    win, wq, wkv = _mla_weights(mla_w_in[0], mla_w_q_up[0], mla_w_kv_up[0])
    q, k, v = _mla_qkv(h, row(mixer_norm[0]), win, row(mla_q_norm[0]), row(mla_kv_norm[0]), wq, wkv, cos, sin)
    o = _attention(q, k, v, nb, seq)
    h = _proj_res(o, mla_w_out[0].astype(BF16), h)
    h = peer(h, 0, False)

    h = _lru_block(h, row(mixer_norm[1]), lru_w_in[0].astype(BF16), lru_conv_w[0].astype(F32),
                   row(lru_conv_b[0]), lru_w_gate_a[0].astype(BF16), row(lru_b_gate_a[0]),
                   lru_w_gate_x[0].astype(BF16), row(lru_b_gate_x[0]), row(lru_lambda[0]),
                   lru_w_out[0].astype(BF16), nb)
    out = peer(h, 1, True)
    return out[:seq * nb].reshape(seq, nb, d).transpose(1, 0, 2)
```

```python
import functools
import math

import jax
import jax.numpy as jnp
from jax import lax
from jax.experimental import pallas as pl
from jax.experimental.pallas import tpu as pltpu

F32 = jnp.float32
BF16 = jnp.bfloat16

N_META = 16
NORM_EPS = 1e-6
MLA_HEADS = 8
QK_NOPE = 128
QK_ROPE = 64
V_HEAD = 128
Q_LORA = 384
KV_LORA = 256
ROPE_THETA = 10000.0
HEAD_PAD = 256
LRU_BLOCKS = 8
LRU_BW = 128
CONV_W = 4
LRU_C = 8.0
PEER_HEADS = 8
N_KEYS = 128
PEER_TOPK = 16
GELU_K0 = math.sqrt(2.0 / math.pi)
GELU_K1 = GELU_K0 * 0.044715
NEG_INF = float("-inf")

ROW_BLOCK = 256
ATTN_BLOCK = 1024
PEER_TOKENS = 768
PEER_EXPERTS = 2048
PEER_SUB = 256
VMEM_LIMIT = 58 * 1024 * 1024


def _dot(a, b):
    return jnp.dot(a, b, preferred_element_type=F32)


def _dot_nt(a, b):
    return lax.dot_general(a, b, (((1,), (1,)), ((), ())), preferred_element_type=F32)


def _rms(x, g):
    return x * lax.rsqrt(jnp.mean(x * x, axis=-1, keepdims=True) + NORM_EPS) * g


def _gelu(x):
    return 0.5 * x * (1.0 + jnp.tanh(x * (GELU_K0 + GELU_K1 * x * x)))


def _params(*sem):
    return pltpu.CompilerParams(dimension_semantics=sem, vmem_limit_bytes=VMEM_LIMIT)


def _full(shape):
    return pl.BlockSpec(shape, lambda *_: (0,) * len(shape))


def _mla_qkv_kernel(h_ref, g_ref, win_ref, qn_ref, kvn_ref, wq_ref, wkv_ref, c_ref, s_ref,
                    q_ref, k_ref, v_ref):
    hn = _rms(h_ref[...], g_ref[...]).astype(BF16)
    a = _dot(hn, win_ref[...])
    q_lat = a[:, :Q_LORA]
    kv_lat = a[:, Q_LORA:Q_LORA + KV_LORA]
    k_pe = a[:, Q_LORA + KV_LORA:Q_LORA + KV_LORA + HEAD_PAD]
    k_pe_rot = a[:, Q_LORA + KV_LORA + HEAD_PAD:]
    cos = c_ref[...]
    sin = s_ref[...]
    k_pe = k_pe * cos + k_pe_rot * sin
    qn = _rms(q_lat, qn_ref[...]).astype(BF16)
    kvn = _rms(kv_lat, kvn_ref[...]).astype(BF16)
    qq = _dot(qn, wq_ref[...])
    kv = _dot(kvn, wkv_ref[...])
    scale = 1.0 / math.sqrt(QK_NOPE + QK_ROPE)
    width = MLA_HEADS * HEAD_PAD
    for h in range(MLA_HEADS):
        lo, hi = h * HEAD_PAD, (h + 1) * HEAD_PAD
        q_h = (qq[:, lo:hi] * cos + qq[:, width + lo:width + hi] * sin) * scale
        q_ref[:, lo:hi] = q_h.astype(BF16)
        k_ref[:, lo:hi] = (kv[:, lo:hi] + k_pe).astype(BF16)
    v_ref[...] = kv[:, width:].astype(BF16)


def _mla_qkv(h, g, win, qn, kvn, wq, wkv, cos, sin):
    t = h.shape[0]
    d = h.shape[1]
    width = MLA_HEADS * HEAD_PAD
    rows = lambda w: pl.BlockSpec((ROW_BLOCK, w), lambda i: (i, 0))
    return pl.pallas_call(
        _mla_qkv_kernel,
        grid=(t // ROW_BLOCK,),
        in_specs=[rows(d), _full(g.shape), _full(win.shape), _full(qn.shape), _full(kvn.shape),
                  _full(wq.shape), _full(wkv.shape), rows(HEAD_PAD), rows(HEAD_PAD)],
        out_specs=[rows(width), rows(width), rows(MLA_HEADS * V_HEAD)],
        out_shape=[jax.ShapeDtypeStruct((t, width), BF16), jax.ShapeDtypeStruct((t, width), BF16),
                   jax.ShapeDtypeStruct((t, MLA_HEADS * V_HEAD), BF16)],
        compiler_params=_params("parallel"),
        name="mla_qkv",
    )(h, g, win, qn, kvn, wq, wkv, cos, sin)


def _attn_kernel(q_ref, k_ref, v_ref, qm_ref, km_ref, vm_ref, o_ref, om_ref, *, blk):
    qi = pl.program_id(1)
    q = q_ref[...]
    km = km_ref[...]
    vm = vm_ref[...]

    s = _dot_nt(q, km)
    m = jnp.max(s, axis=1, keepdims=True)
    p = jnp.exp(s - m)
    l = jnp.sum(p, axis=1, keepdims=True)
    acc = _dot(p.astype(BF16), vm)

    def step(j, carry, diagonal):
        m, l, acc = carry
        off = pl.multiple_of(j * blk, blk)
        s = _dot_nt(q, k_ref[pl.ds(off, blk), :])
        if diagonal:
            row = lax.broadcasted_iota(jnp.int32, (blk, blk), 0)
            col = lax.broadcasted_iota(jnp.int32, (blk, blk), 1)
            s = jnp.where(col <= row, s, -1e30)
        m_new = jnp.maximum(m, jnp.max(s, axis=1, keepdims=True))
        alpha = jnp.exp(m - m_new)
        p = jnp.exp(s - m_new)
        l = alpha * l + jnp.sum(p, axis=1, keepdims=True)
        acc = alpha * acc + _dot(p.astype(BF16), v_ref[pl.ds(off, blk), :])
        return m_new, l, acc

    carry = lax.fori_loop(0, qi, lambda j, c: step(j, c, False), (m, l, acc))
    m, l, acc = step(qi, carry, True)
    o_ref[...] = (acc / l).astype(BF16)

    @pl.when(qi == 0)
    def _meta_queries():
        s = _dot_nt(qm_ref[...], km)
        row = lax.broadcasted_iota(jnp.int32, (N_META, N_META), 0)
        col = lax.broadcasted_iota(jnp.int32, (N_META, N_META), 1)
        s = jnp.where(col <= row, s, -1e30)
        p = jnp.exp(s - jnp.max(s, axis=1, keepdims=True))
        o = _dot(p.astype(BF16), vm) / jnp.sum(p, axis=1, keepdims=True)
        om_ref[...] = o.astype(BF16)


def _attention(q, k, v, nb, seq):
    width = MLA_HEADS * HEAD_PAD
    vwidth = MLA_HEADS * V_HEAD
    steps = seq + N_META
    q2 = q.reshape(steps, nb * width)
    k2 = k.reshape(steps, nb * width)
    v2 = v.reshape(steps, nb * vwidth)
    blk = min(ATTN_BLOCK, seq)
    meta_blk = seq // N_META
    o, om = pl.pallas_call(
        functools.partial(_attn_kernel, blk=blk),
        grid=(nb * MLA_HEADS, seq // blk),
        in_specs=[pl.BlockSpec((blk, HEAD_PAD), lambda bh, qi: (qi, bh)),
                  pl.BlockSpec((seq, HEAD_PAD), lambda bh, qi: (0, bh)),
                  pl.BlockSpec((seq, V_HEAD), lambda bh, qi: (0, bh)),
                  pl.BlockSpec((N_META, HEAD_PAD), lambda bh, qi: (meta_blk, bh)),
                  pl.BlockSpec((N_META, HEAD_PAD), lambda bh, qi: (meta_blk, bh)),
                  pl.BlockSpec((N_META, V_HEAD), lambda bh, qi: (meta_blk, bh))],
        out_specs=[pl.BlockSpec((blk, V_HEAD), lambda bh, qi: (qi, bh)),
                   pl.BlockSpec((N_META, V_HEAD), lambda bh, qi: (0, bh))],
        out_shape=[jax.ShapeDtypeStruct((seq, nb * vwidth), BF16),
                   jax.ShapeDtypeStruct((N_META, nb * vwidth), BF16)],
        compiler_params=_params("parallel", "arbitrary"),
        name="mla_attention",
    )(q2, k2, v2, q2, k2, v2)
    return jnp.concatenate([o.reshape(seq * nb, vwidth), om.reshape(N_META * nb, vwidth)], axis=0)


def _proj_res_kernel(o_ref, w_ref, h_ref, out_ref):
    out_ref[...] = h_ref[...] + _dot(o_ref[...], w_ref[...])


def _proj_res(o, w, h):
    t, d = h.shape
    rows = lambda width: pl.BlockSpec((ROW_BLOCK, width), lambda i: (i, 0))
    return pl.pallas_call(
        _proj_res_kernel,
        grid=(t // ROW_BLOCK,),
        in_specs=[rows(o.shape[1]), _full(w.shape), rows(d)],
        out_specs=rows(d),
        out_shape=jax.ShapeDtypeStruct((t, d), F32),
        compiler_params=_params("parallel"),
        name="mla_out_proj",
    )(o, w, h)


def _lru_kernel(h_ref, g_ref, win_ref, cw_ref, cb_ref, wa_ref, ba_ref, wx_ref, bx_ref, lam_ref,
                wout_ref, out_ref, xbuf, state, *, nb):
    rows, width = h_ref.shape
    hist = (CONV_W - 1) * nb

    @pl.when(pl.program_id(0) == 0)
    def _init():
        xbuf[0:hist, :] = jnp.zeros((hist, width), F32)
        state[...] = jnp.zeros_like(state)

    x = h_ref[...]
    hn = _rms(x, g_ref[...]).astype(BF16)
    proj = _dot(hn, win_ref[...])
    gate = _gelu(proj[:, :width])
    xr = proj[:, width:]
    xbuf[hist:hist + rows, :] = xr
    xc = cb_ref[...] + xr * cw_ref[CONV_W - 1:CONV_W, :]
    for j in range(CONV_W - 1):
        xc = xc + xbuf[j * nb:j * nb + rows, :] * cw_ref[j:j + 1, :]
    xbuf[0:hist, :] = xbuf[rows:rows + hist, :]

    xcb = xc.astype(BF16)
    ra, ix = [], []
    for n in range(LRU_BLOCKS):
        blk = xcb[:, n * LRU_BW:(n + 1) * LRU_BW]
        ra.append(_dot(blk, wa_ref[n]))
        ix.append(_dot(blk, wx_ref[n]))
    r = jax.nn.sigmoid(jnp.concatenate(ra, axis=1) + ba_ref[...])
    i = jax.nn.sigmoid(jnp.concatenate(ix, axis=1) + bx_ref[...])
    lam = lam_ref[...]
    softplus = jnp.maximum(-lam, 0.0) + jnp.log1p(jnp.exp(-jnp.abs(lam)))
    log_a = -LRU_C * r * softplus
    a = jnp.exp(log_a)
    th = jnp.tanh(log_a)
    b = jnp.sqrt(-2.0 * th / (1.0 - th)) * (i * xc)

    hcur = state[...]
    hs = []
    for s in range(rows // nb):
        hcur = a[s * nb:(s + 1) * nb] * hcur + b[s * nb:(s + 1) * nb]
        hs.append(hcur)
    state[...] = hcur
    y = (jnp.concatenate(hs, axis=0) * gate).astype(BF16)
    out_ref[...] = x + _dot(y, wout_ref[...])


def _lru_block(h, g, win, cw, cb, wa, ba, wx, bx, lam, wout, nb):
    t, d = h.shape
    nblk = t // ROW_BLOCK
    rows = pl.BlockSpec((ROW_BLOCK, d), lambda i: ((i + nblk - 1) % nblk, 0))
    args = (g, win, cw, cb, wa, ba, wx, bx, lam, wout)
    return pl.pallas_call(
        functools.partial(_lru_kernel, nb=nb),
        grid=(nblk,),
        in_specs=[rows] + [_full(a.shape) for a in args],
        out_specs=rows,
        out_shape=jax.ShapeDtypeStruct((t, d), F32),
        scratch_shapes=[pltpu.VMEM(((CONV_W - 1) * nb + ROW_BLOCK, d), F32), pltpu.VMEM((nb, d), F32)],
        compiler_params=_params("arbitrary"),
        name="rglru_block",
    )(h, *args)


LANE_TILE = 256
TOPK_LANES = 128
ROW_TILE = 64
RANK_CODE_BASE = 1024.0
RANK_CODE_SCALE = 2.0 ** 90
RANK_NONE = 31.0


def _col_max(tiles):
    while len(tiles) > 1:
        tiles = [jnp.maximum(a, b) for a, b in zip(tiles[0::2], tiles[1::2])] + tiles[len(tiles) & ~1:]
    m = tiles[0]
    for shift in (4, 2, 1):
        m = jnp.maximum(m, pltpu.roll(m, shift, axis=0))
    return m


def _extract_top(tiles, count, emit, rank_codes=False):
    for rank in range(count):
        m = _col_max(tiles)
        emit(rank, m)
        if rank_codes:
            code = -(RANK_CODE_BASE + rank) * RANK_CODE_SCALE
            tiles = [jnp.where(t == m, code, t) for t in tiles]
        elif rank + 1 < count:
            tiles = [jnp.where(t == m, NEG_INF, t) for t in tiles]
    return tiles


def _pair_words(x):
    bits = pltpu.bitcast(x, jnp.uint32)
    return bits | (bits >> 16)


def _peer_kernel(h_ref, g_ref, wqt_ref, sk_ref, u_ref, vt_ref, og_ref, out_ref,
                 xn_s, s_s, vals_s, rep_s, rk_s, e1_s, n_s, e0_s, rows_s, st_s, wt_s, acc_s, *, norm_output):
    c = pl.program_id(1)
    tb = h_ref.shape[0]
    nsub = u_ref.shape[0] // PEER_SUB
    per_sub = PEER_SUB // N_KEYS
    ntop = PEER_TOPK + 1
    vrows = vals_s.shape[1]
    key_tiles = N_KEYS // 8
    lane_tiles = [slice(k * LANE_TILE, (k + 1) * LANE_TILE) for k in range(tb // LANE_TILE)]

    @pl.when(c == 0)
    def _prepare():
        xn_s[...] = _rms(h_ref[...], g_ref[...]).T.astype(BF16)

        def head_scores(h, buf):
            for p in range(2):
                w = wqt_ref[pl.ds(pl.multiple_of((2 * h + p) * N_KEYS, N_KEYS), N_KEYS), :]
                s_s[buf, p] = _dot(sk_ref[p], _dot(w, xn_s[...]).astype(BF16))

        def head_tables(h, buf):
            for ls in [slice(k * TOPK_LANES, (k + 1) * TOPK_LANES) for k in range(tb // TOPK_LANES)]:
                top0 = {}
                neg = jnp.full((vrows - ntop, TOPK_LANES), NEG_INF, F32)

                def emit0(rank, m):
                    top0[rank] = m
                    vals_s[0, rank:rank + 1, ls] = m[0:1]

                def emit1(rank, m):
                    rep_s[rank, :, ls] = m
                    vals_s[1, rank:rank + 1, ls] = m[0:1]

                _extract_top([s_s[buf, 0, 8 * k:8 * k + 8, ls] for k in range(key_tiles)], ntop, emit0)
                coded = _extract_top([s_s[buf, 1, 8 * k:8 * k + 8, ls] for k in range(key_tiles)], ntop, emit1,
                                     rank_codes=True)
                vals_s[0, ntop:vrows, ls] = neg
                vals_s[1, ntop:vrows, ls] = neg
                v0 = vals_s[0, :, ls]
                v1 = vals_s[1, :, ls]
                cand = ([top0[0] + v1[8 * k:8 * k + 8] for k in range(vrows // 8)]
                        + [top0[a] + v1[0:8] for a in range(1, 8)]
                        + [v0[8 * k:8 * k + 8] + rep_s[0, :, ls] for k in range(1, vrows // 8)])
                got = {}

                def emit(rank, m):
                    got[rank] = m
                _extract_top(cand, ntop, emit)
                z = sum(jnp.exp(got[k] - got[0]) for k in range(PEER_TOPK))
                thr = 0.5 * (got[PEER_TOPK - 1] + got[PEER_TOPK])
                scale = 0.5 / z
                for k in range(0, key_tiles, 2):
                    rows = slice(8 * k, 8 * k + 16)
                    s0 = s_s[buf, 0, rows, ls]
                    slack = jnp.concatenate([thr, thr], axis=0) - s0
                    n = jnp.zeros_like(s0)
                    for b in range(ntop):
                        vb = rep_s[b, :, ls]
                        n = jnp.where(jnp.concatenate([vb, vb], axis=0) >= slack, float(b + 1), n)
                    n_s[k // 2, h, :, ls] = _pair_words(n)
                    m0 = jnp.concatenate([top0[0], top0[0]], axis=0)
                    e0 = jnp.exp(s0 - m0) * jnp.concatenate([scale, scale], axis=0)
                    e0_s[k // 2, h, :, ls] = _pair_words(e0.astype(BF16).astype(F32))
                    m1 = rep_s[0, :, ls]
                    e1 = jnp.exp(s_s[buf, 1, rows, ls] - jnp.concatenate([m1, m1], axis=0))
                    e1_s[h, rows, ls] = e1.astype(BF16)
                    t = jnp.concatenate([coded[k], coded[k + 1]], axis=0)
                    rank = jnp.where(t <= -0.5 * RANK_CODE_BASE * RANK_CODE_SCALE,
                                     t * (-1.0 / RANK_CODE_SCALE) - RANK_CODE_BASE, RANK_NONE)
                    rk_s[h, rows, ls] = rank.astype(BF16)

        def head_pair(m, carry):
            h = 2 * m
            head_scores(h + 1, 1)
            head_tables(h, 0)
            head_scores(jnp.minimum(h + 2, PEER_HEADS - 1), 0)
            head_tables(h + 1, 1)
            return carry

        head_scores(0, 0)
        lax.fori_loop(0, PEER_HEADS // 2, head_pair, 0)
        acc_s[...] = jnp.zeros_like(acc_s)

    def scores(sub):
        st_s[sub % 2] = _dot(u_ref[sub * PEER_SUB:(sub + 1) * PEER_SUB, :], xn_s[...])

    def gates(sub):
        buf = sub % 2
        for h in range(PEER_HEADS):
            for ii in range(per_sub):
                row = sub * per_sub + ii
                rows_s[buf, h, ii, 0] = jnp.broadcast_to(n_s[c, h, row:row + 1, :], (8, tb))
                rows_s[buf, h, ii, 1] = jnp.broadcast_to(e0_s[c, h, row:row + 1, :], (8, tb))
        reps = ROW_TILE // 16
        for ls in lane_tiles:
            for jt in range(N_KEYS // ROW_TILE):
                js = slice(jt * ROW_TILE, (jt + 1) * ROW_TILE)
                gate = [jnp.zeros((ROW_TILE, LANE_TILE), BF16) for _ in range(per_sub)]
                for h in range(PEER_HEADS):
                    rk = rk_s[h, js, ls]
                    e1 = e1_s[h, js, ls]
                    for ii in range(per_sub):
                        n = jnp.tile(pltpu.bitcast(rows_s[buf, h, ii, 0, :, ls], BF16), (reps, 1))
                        e = jnp.tile(pltpu.bitcast(rows_s[buf, h, ii, 1, :, ls], BF16), (reps, 1))
                        gate[ii] = gate[ii] + e * jnp.where(rk < n, e1, jnp.zeros_like(e1))
                for ii in range(per_sub):
                    lo = ii * N_KEYS + jt * ROW_TILE
                    s = st_s[buf, lo:lo + ROW_TILE, ls]
                    act = s + s * jnp.tanh(s * (GELU_K0 + GELU_K1 * s * s))
                    wt_s[buf, lo:lo + ROW_TILE, ls] = act.astype(BF16) * gate[ii]

    scores(0)
    for sub in range(nsub):
        if sub + 1 < nsub:
            scores(sub + 1)
        gates(sub)
        acc_s[...] += _dot(vt_ref[:, sub * PEER_SUB:(sub + 1) * PEER_SUB], wt_s[sub % 2])

    @pl.when(c == pl.num_programs(1) - 1)
    def _finish():
        out = h_ref[...] + acc_s[...].T
        out_ref[...] = _rms(out, og_ref[...]) if norm_output else out


def _peer(h, g, wqt, sk, u, vt, out_gain, norm_output):
    t, d = h.shape
    tb = PEER_TOKENS
    n_exp = u.shape[0]
    vrows = 24
    keys_per_step = PEER_EXPERTS // N_KEYS
    assert keys_per_step == 16
    step_tables = (N_KEYS // keys_per_step, PEER_HEADS, keys_per_step, tb)
    return pl.pallas_call(
        functools.partial(_peer_kernel, norm_output=norm_output),
        grid=(t // tb, n_exp // PEER_EXPERTS),
        in_specs=[pl.BlockSpec((tb, d), lambda i, c: (i, 0), pipeline_mode=pl.Buffered(1)),
                  _full(g.shape),
                  pl.BlockSpec(wqt.shape, lambda i, c: (0, 0), pipeline_mode=pl.Buffered(1)),
                  _full(sk.shape),
                  pl.BlockSpec((PEER_EXPERTS, d), lambda i, c: (c, 0)),
                  pl.BlockSpec((d, PEER_EXPERTS), lambda i, c: (0, c)),
                  _full(out_gain.shape)],
        out_specs=pl.BlockSpec((tb, d), lambda i, c: (i, 0)),
        out_shape=jax.ShapeDtypeStruct((t, d), F32),
        scratch_shapes=[pltpu.VMEM((d, tb), BF16),
                        pltpu.VMEM((2, 2, N_KEYS, tb), F32),
                        pltpu.VMEM((2, vrows, tb), F32),
                        pltpu.VMEM((vrows, 8, tb), F32),
                        pltpu.VMEM((PEER_HEADS, N_KEYS, tb), BF16),
                        pltpu.VMEM((PEER_HEADS, N_KEYS, tb), BF16),
                        pltpu.VMEM(step_tables, jnp.uint32),
                        pltpu.VMEM(step_tables, jnp.uint32),
                        pltpu.VMEM((2, PEER_HEADS, PEER_SUB // N_KEYS, 2, 8, tb), jnp.uint32),
                        pltpu.VMEM((2, PEER_SUB, tb), F32),
                        pltpu.VMEM((2, PEER_SUB, tb), BF16),
                        pltpu.VMEM((d, tb), F32)],
        compiler_params=_params("parallel", "arbitrary"),
        name="peer_ffn",
    )(h, g, wqt, sk, u, vt, out_gain)


def _rot_cols(w):
    half = QK_ROPE // 2
    return jnp.concatenate([-w[:, half:], w[:, :half]], axis=1)


def _pad_rope(w):
    z = jnp.zeros((w.shape[0], QK_NOPE), w.dtype)
    z2 = jnp.zeros((w.shape[0], HEAD_PAD - QK_NOPE - QK_ROPE), w.dtype)
    return jnp.concatenate([z, w, z2], axis=1)


def _mla_weights(w_in, w_q_up, w_kv_up):
    w_kr = w_in[:, Q_LORA + KV_LORA:]
    win = jnp.concatenate([w_in[:, :Q_LORA + KV_LORA], _pad_rope(w_kr), _pad_rope(_rot_cols(w_kr))], axis=1)
    wq = w_q_up.reshape(Q_LORA, MLA_HEADS, QK_NOPE + QK_ROPE)
    pad = jnp.zeros((Q_LORA, MLA_HEADS, HEAD_PAD - QK_NOPE - QK_ROPE), w_q_up.dtype)
    nope0 = jnp.zeros((Q_LORA, MLA_HEADS, QK_NOPE), w_q_up.dtype)
    rope = wq[:, :, QK_NOPE:]
    half = QK_ROPE // 2
    rope_rot = jnp.concatenate([-rope[:, :, half:], rope[:, :, :half]], axis=2)
    q_plain = jnp.concatenate([wq, pad], axis=2).reshape(Q_LORA, MLA_HEADS * HEAD_PAD)
    q_rot = jnp.concatenate([nope0, rope_rot, pad], axis=2).reshape(Q_LORA, MLA_HEADS * HEAD_PAD)
    wkv = w_kv_up.reshape(KV_LORA, MLA_HEADS, QK_NOPE + V_HEAD)
    kpad = jnp.zeros((KV_LORA, MLA_HEADS, HEAD_PAD - QK_NOPE), w_kv_up.dtype)
    k_nope = jnp.concatenate([wkv[:, :, :QK_NOPE], kpad], axis=2).reshape(KV_LORA, MLA_HEADS * HEAD_PAD)
    v = wkv[:, :, QK_NOPE:].reshape(KV_LORA, MLA_HEADS * V_HEAD)
    return (win.astype(BF16), jnp.concatenate([q_plain, q_rot], axis=1).astype(BF16),
            jnp.concatenate([k_nope, v], axis=1).astype(BF16))


def _rope_tables(pos):
    half = QK_ROPE // 2
    freqs = ROPE_THETA ** (-jnp.arange(half, dtype=F32) / half)
    ang = pos.astype(F32)[:, None] * freqs
    n = pos.shape[0]
    ones = jnp.ones((n, QK_NOPE), F32)
    zeros = jnp.zeros((n, QK_NOPE), F32)
    tail = jnp.zeros((n, HEAD_PAD - QK_NOPE - QK_ROPE), F32)
    cos = jnp.concatenate([ones, jnp.cos(ang), jnp.cos(ang), tail], axis=1)
    sin = jnp.concatenate([zeros, jnp.sin(ang), jnp.sin(ang), tail], axis=1)
    return cos, sin


def kernel(x, positions, meta_tokens, mixer_norm, ffn_norm, final_norm, mla_w_in, mla_q_norm, mla_w_q_up, mla_kv_norm, mla_w_kv_up, mla_w_out, lru_w_in, lru_conv_w, lru_conv_b, lru_w_gate_a, lru_b_gate_a, lru_w_gate_x, lru_b_gate_x, lru_lambda, lru_w_out, peer_w_query, peer_sub_keys, peer_u, peer_v):
    nb, seq, d = x.shape
    assert nb % 8 == 0 and (nb * N_META) % ROW_BLOCK == 0 and (nb * seq) % ROW_BLOCK == 0
    assert (nb * (seq + N_META)) % PEER_TOKENS == 0 and seq % min(ATTN_BLOCK, seq) == 0
    row = lambda v: v.reshape(1, -1).astype(F32)

    h = jnp.concatenate([x.transpose(1, 0, 2).reshape(seq * nb, d),
                         jnp.repeat(meta_tokens.astype(x.dtype), nb, axis=0)], axis=0)
    pos = jnp.concatenate([positions.astype(jnp.int32).T.reshape(-1) + N_META,
                           jnp.repeat(jnp.arange(N_META, dtype=jnp.int32), nb)], axis=0)
    cos, sin = _rope_tables(pos)

    def peer(h, layer, norm_output):
        wqt = peer_w_query[layer].T.astype(BF16)
        return _peer(h, row(ffn_norm[layer]), wqt, peer_sub_keys[layer].astype(BF16),
                     peer_u[layer].astype(BF16), peer_v[layer].astype(BF16).T, row(final_norm), norm_output)

    win, wq, wkv = _mla_weights(mla_w_in[0], mla_w_q_up[0], mla_w_kv_up[0])
    q, k, v = _mla_qkv(h, row(mixer_norm[0]), win, row(mla_q_norm[0]), row(mla_kv_norm[0]), wq, wkv, cos, sin)
    o = _attention(q, k, v, nb, seq)
    h = _proj_res(o, mla_w_out[0].astype(BF16), h)
    h = peer(h, 0, False)

    h = _lru_block(h, row(mixer_norm[1]), lru_w_in[0].astype(BF16), lru_conv_w[0].astype(F32),
                   row(lru_conv_b[0]), lru_w_gate_a[0].astype(BF16), row(lru_b_gate_a[0]),
                   lru_w_gate_x[0].astype(BF16), row(lru_b_gate_x[0]), row(lru_lambda[0]),
                   lru_w_out[0].astype(BF16), nb)
    out = peer(h, 1, True)
    return out[:seq * nb].reshape(seq, nb, d).transpose(1, 0, 2)
```

```python
import functools
import math

import jax
import jax.numpy as jnp
from jax import lax
from jax.experimental import pallas as pl
from jax.experimental.pallas import tpu as pltpu

F32 = jnp.float32
BF16 = jnp.bfloat16

N_META = 16
NORM_EPS = 1e-6
MLA_HEADS = 8
QK_NOPE = 128
QK_ROPE = 64
V_HEAD = 128
Q_LORA = 384
KV_LORA = 256
ROPE_THETA = 10000.0
HEAD_PAD = 256
LRU_BLOCKS = 8
LRU_BW = 128
CONV_W = 4
LRU_C = 8.0
PEER_HEADS = 8
N_KEYS = 128
PEER_TOPK = 16
GELU_K0 = math.sqrt(2.0 / math.pi)
GELU_K1 = GELU_K0 * 0.044715
NEG_INF = float("-inf")

ROW_BLOCK = 256
ATTN_BLOCK = 1024
PEER_TOKENS = 768
PEER_EXPERTS = 2048
PEER_SUB = 256
VMEM_LIMIT = 58 * 1024 * 1024


def _dot(a, b):
    return jnp.dot(a, b, preferred_element_type=F32)


def _dot_nt(a, b):
    return lax.dot_general(a, b, (((1,), (1,)), ((), ())), preferred_element_type=F32)


def _rms(x, g):
    return x * lax.rsqrt(jnp.mean(x * x, axis=-1, keepdims=True) + NORM_EPS) * g


def _gelu(x):
    return 0.5 * x * (1.0 + jnp.tanh(x * (GELU_K0 + GELU_K1 * x * x)))


def _params(*sem):
    return pltpu.CompilerParams(dimension_semantics=sem, vmem_limit_bytes=VMEM_LIMIT)


def _full(shape):
    return pl.BlockSpec(shape, lambda *_: (0,) * len(shape))


def _mla_qkv_kernel(h_ref, g_ref, win_ref, qn_ref, kvn_ref, wq_ref, wkv_ref, c_ref, s_ref,
                    q_ref, k_ref, v_ref):
    hn = _rms(h_ref[...], g_ref[...]).astype(BF16)
    a = _dot(hn, win_ref[...])
    q_lat = a[:, :Q_LORA]
    kv_lat = a[:, Q_LORA:Q_LORA + KV_LORA]
    k_pe = a[:, Q_LORA + KV_LORA:Q_LORA + KV_LORA + HEAD_PAD]
    k_pe_rot = a[:, Q_LORA + KV_LORA + HEAD_PAD:]
    cos = c_ref[...]
    sin = s_ref[...]
    k_pe = k_pe * cos + k_pe_rot * sin
    qn = _rms(q_lat, qn_ref[...]).astype(BF16)
    kvn = _rms(kv_lat, kvn_ref[...]).astype(BF16)
    qq = _dot(qn, wq_ref[...])
    kv = _dot(kvn, wkv_ref[...])
    scale = 1.0 / math.sqrt(QK_NOPE + QK_ROPE)
    width = MLA_HEADS * HEAD_PAD
    for h in range(MLA_HEADS):
        lo, hi = h * HEAD_PAD, (h + 1) * HEAD_PAD
        q_h = (qq[:, lo:hi] * cos + qq[:, width + lo:width + hi] * sin) * scale
        q_ref[:, lo:hi] = q_h.astype(BF16)
        k_ref[:, lo:hi] = (kv[:, lo:hi] + k_pe).astype(BF16)
    v_ref[...] = kv[:, width:].astype(BF16)


def _mla_qkv(h, g, win, qn, kvn, wq, wkv, cos, sin):
    t = h.shape[0]
    d = h.shape[1]
    width = MLA_HEADS * HEAD_PAD
    rows = lambda w: pl.BlockSpec((ROW_BLOCK, w), lambda i: (i, 0))
    return pl.pallas_call(
        _mla_qkv_kernel,
        grid=(t // ROW_BLOCK,),
        in_specs=[rows(d), _full(g.shape), _full(win.shape), _full(qn.shape), _full(kvn.shape),
                  _full(wq.shape), _full(wkv.shape), rows(HEAD_PAD), rows(HEAD_PAD)],
        out_specs=[rows(width), rows(width), rows(MLA_HEADS * V_HEAD)],
        out_shape=[jax.ShapeDtypeStruct((t, width), BF16), jax.ShapeDtypeStruct((t, width), BF16),
                   jax.ShapeDtypeStruct((t, MLA_HEADS * V_HEAD), BF16)],
        compiler_params=_params("parallel"),
        name="mla_qkv",
    )(h, g, win, qn, kvn, wq, wkv, cos, sin)


def _attn_kernel(q_ref, k_ref, v_ref, qm_ref, km_ref, vm_ref, o_ref, om_ref, *, blk):
    qi = pl.program_id(1)
    q = q_ref[...]
    km = km_ref[...]
    vm = vm_ref[...]

    s = _dot_nt(q, km)
    m = jnp.max(s, axis=1, keepdims=True)
    p = jnp.exp(s - m)
    l = jnp.sum(p, axis=1, keepdims=True)
    acc = _dot(p.astype(BF16), vm)

    def step(j, carry, diagonal):
        m, l, acc = carry
        off = pl.multiple_of(j * blk, blk)
        s = _dot_nt(q, k_ref[pl.ds(off, blk), :])
        if diagonal:
            row = lax.broadcasted_iota(jnp.int32, (blk, blk), 0)
            col = lax.broadcasted_iota(jnp.int32, (blk, blk), 1)
            s = jnp.where(col <= row, s, -1e30)
        m_new = jnp.maximum(m, jnp.max(s, axis=1, keepdims=True))
        alpha = jnp.exp(m - m_new)
        p = jnp.exp(s - m_new)
        l = alpha * l + jnp.sum(p, axis=1, keepdims=True)
        acc = alpha * acc + _dot(p.astype(BF16), v_ref[pl.ds(off, blk), :])
        return m_new, l, acc

    carry = lax.fori_loop(0, qi, lambda j, c: step(j, c, False), (m, l, acc))
    m, l, acc = step(qi, carry, True)
    o_ref[...] = (acc / l).astype(BF16)

    @pl.when(qi == 0)
    def _meta_queries():
        s = _dot_nt(qm_ref[...], km)
        row = lax.broadcasted_iota(jnp.int32, (N_META, N_META), 0)
        col = lax.broadcasted_iota(jnp.int32, (N_META, N_META), 1)
        s = jnp.where(col <= row, s, -1e30)
        p = jnp.exp(s - jnp.max(s, axis=1, keepdims=True))
        o = _dot(p.astype(BF16), vm) / jnp.sum(p, axis=1, keepdims=True)
        om_ref[...] = o.astype(BF16)


def _attention(q, k, v, nb, seq):
    width = MLA_HEADS * HEAD_PAD
    vwidth = MLA_HEADS * V_HEAD
    steps = seq + N_META
    q2 = q.reshape(steps, nb * width)
    k2 = k.reshape(steps, nb * width)
    v2 = v.reshape(steps, nb * vwidth)
    blk = min(ATTN_BLOCK, seq)
    meta_blk = seq // N_META
    o, om = pl.pallas_call(
        functools.partial(_attn_kernel, blk=blk),
        grid=(nb * MLA_HEADS, seq // blk),
        in_specs=[pl.BlockSpec((blk, HEAD_PAD), lambda bh, qi: (qi, bh)),
                  pl.BlockSpec((seq, HEAD_PAD), lambda bh, qi: (0, bh)),
                  pl.BlockSpec((seq, V_HEAD), lambda bh, qi: (0, bh)),
                  pl.BlockSpec((N_META, HEAD_PAD), lambda bh, qi: (meta_blk, bh)),
                  pl.BlockSpec((N_META, HEAD_PAD), lambda bh, qi: (meta_blk, bh)),
                  pl.BlockSpec((N_META, V_HEAD), lambda bh, qi: (meta_blk, bh))],
        out_specs=[pl.BlockSpec((blk, V_HEAD), lambda bh, qi: (qi, bh)),
                   pl.BlockSpec((N_META, V_HEAD), lambda bh, qi: (0, bh))],
        out_shape=[jax.ShapeDtypeStruct((seq, nb * vwidth), BF16),
                   jax.ShapeDtypeStruct((N_META, nb * vwidth), BF16)],
        compiler_params=_params("parallel", "arbitrary"),
        name="mla_attention",
    )(q2, k2, v2, q2, k2, v2)
    return jnp.concatenate([o.reshape(seq * nb, vwidth), om.reshape(N_META * nb, vwidth)], axis=0)


def _proj_res_kernel(o_ref, w_ref, h_ref, out_ref):
    out_ref[...] = h_ref[...] + _dot(o_ref[...], w_ref[...])


def _proj_res(o, w, h):
    t, d = h.shape
    rows = lambda width: pl.BlockSpec((ROW_BLOCK, width), lambda i: (i, 0))
    return pl.pallas_call(
        _proj_res_kernel,
        grid=(t // ROW_BLOCK,),
        in_specs=[rows(o.shape[1]), _full(w.shape), rows(d)],
        out_specs=rows(d),
        out_shape=jax.ShapeDtypeStruct((t, d), F32),
        compiler_params=_params("parallel"),
        name="mla_out_proj",
    )(o, w, h)


def _lru_kernel(h_ref, g_ref, win_ref, cw_ref, cb_ref, wa_ref, ba_ref, wx_ref, bx_ref, lam_ref,
                wout_ref, out_ref, xbuf, state, *, nb):
    rows, width = h_ref.shape
    hist = (CONV_W - 1) * nb

    @pl.when(pl.program_id(0) == 0)
    def _init():
        xbuf[0:hist, :] = jnp.zeros((hist, width), F32)
        state[...] = jnp.zeros_like(state)

    x = h_ref[...]
    hn = _rms(x, g_ref[...]).astype(BF16)
    proj = _dot(hn, win_ref[...])
    gate = _gelu(proj[:, :width])
    xr = proj[:, width:]
    xbuf[hist:hist + rows, :] = xr
    xc = cb_ref[...] + xr * cw_ref[CONV_W - 1:CONV_W, :]
    for j in range(CONV_W - 1):
        xc = xc + xbuf[j * nb:j * nb + rows, :] * cw_ref[j:j + 1, :]
    xbuf[0:hist, :] = xbuf[rows:rows + hist, :]

    xcb = xc.astype(BF16)
    ra, ix = [], []
    for n in range(LRU_BLOCKS):
        blk = xcb[:, n * LRU_BW:(n + 1) * LRU_BW]
        ra.append(_dot(blk, wa_ref[n]))
        ix.append(_dot(blk, wx_ref[n]))
    r = jax.nn.sigmoid(jnp.concatenate(ra, axis=1) + ba_ref[...])
    i = jax.nn.sigmoid(jnp.concatenate(ix, axis=1) + bx_ref[...])
    lam = lam_ref[...]
    softplus = jnp.maximum(-lam, 0.0) + jnp.log1p(jnp.exp(-jnp.abs(lam)))
    log_a = -LRU_C * r * softplus
    a = jnp.exp(log_a)
    th = jnp.tanh(log_a)
    b = jnp.sqrt(-2.0 * th / (1.0 - th)) * (i * xc)

    hcur = state[...]
    hs = []
    for s in range(rows // nb):
        hcur = a[s * nb:(s + 1) * nb] * hcur + b[s * nb:(s + 1) * nb]
        hs.append(hcur)
    state[...] = hcur
    y = (jnp.concatenate(hs, axis=0) * gate).astype(BF16)
    out_ref[...] = x + _dot(y, wout_ref[...])


def _lru_block(h, g, win, cw, cb, wa, ba, wx, bx, lam, wout, nb):
    t, d = h.shape
    nblk = t // ROW_BLOCK
    rows = pl.BlockSpec((ROW_BLOCK, d), lambda i: ((i + nblk - 1) % nblk, 0))
    args = (g, win, cw, cb, wa, ba, wx, bx, lam, wout)
    return pl.pallas_call(
        functools.partial(_lru_kernel, nb=nb),
        grid=(nblk,),
        in_specs=[rows] + [_full(a.shape) for a in args],
        out_specs=rows,
        out_shape=jax.ShapeDtypeStruct((t, d), F32),
        scratch_shapes=[pltpu.VMEM(((CONV_W - 1) * nb + ROW_BLOCK, d), F32), pltpu.VMEM((nb, d), F32)],
        compiler_params=_params("arbitrary"),
        name="rglru_block",
    )(h, *args)


LANE_TILE = 256
TOPK_LANES = 128
ROW_TILE = 64
RANK_CODE_BASE = 1024.0
RANK_CODE_SCALE = 2.0 ** 90
RANK_NONE = 31.0


def _col_max(tiles):
    while len(tiles) > 1:
        tiles = [jnp.maximum(a, b) for a, b in zip(tiles[0::2], tiles[1::2])] + tiles[len(tiles) & ~1:]
    m = tiles[0]
    for shift in (4, 2, 1):
        m = jnp.maximum(m, pltpu.roll(m, shift, axis=0))
    return m


def _extract_top(tiles, count, emit, rank_codes=False):
    for rank in range(count):
        m = _col_max(tiles)
        emit(rank, m)
        if rank_codes:
            code = -(RANK_CODE_BASE + rank) * RANK_CODE_SCALE
            tiles = [jnp.where(t == m, code, t) for t in tiles]
        elif rank + 1 < count:
            tiles = [jnp.where(t == m, NEG_INF, t) for t in tiles]
    return tiles


def _pair_words(x):
    bits = pltpu.bitcast(x, jnp.uint32)
    return bits | (bits >> 16)


def _peer_kernel(h_ref, g_ref, wqt_ref, sk_ref, u_ref, vt_ref, og_ref, out_ref,
                 xn_s, s_s, vals_s, rep_s, rk_s, e1_s, n_s, e0_s, rows_s, st_s, wt_s, acc_s, *, norm_output):
    c = pl.program_id(1)
    tb = h_ref.shape[0]
    nsub = u_ref.shape[0] // PEER_SUB
    per_sub = PEER_SUB // N_KEYS
    ntop = PEER_TOPK + 1
    vrows = vals_s.shape[1]
    key_tiles = N_KEYS // 8
    lane_tiles = [slice(k * LANE_TILE, (k + 1) * LANE_TILE) for k in range(tb // LANE_TILE)]

    @pl.when(c == 0)
    def _prepare():
        xn_s[...] = _rms(h_ref[...], g_ref[...]).T.astype(BF16)

        def head_scores(h, buf):
            for p in range(2):
                w = wqt_ref[pl.ds(pl.multiple_of((2 * h + p) * N_KEYS, N_KEYS), N_KEYS), :]
                s_s[buf, p] = _dot(sk_ref[p], _dot(w, xn_s[...]).astype(BF16))

        def head_tables(h, buf):
            for ls in [slice(k * TOPK_LANES, (k + 1) * TOPK_LANES) for k in range(tb // TOPK_LANES)]:
                top0 = {}
                neg = jnp.full((vrows - ntop, TOPK_LANES), NEG_INF, F32)

                def emit0(rank, m):
                    top0[rank] = m
                    vals_s[0, rank:rank + 1, ls] = m[0:1]

                def emit1(rank, m):
                    rep_s[rank, :, ls] = m
                    vals_s[1, rank:rank + 1, ls] = m[0:1]

                _extract_top([s_s[buf, 0, 8 * k:8 * k + 8, ls] for k in range(key_tiles)], ntop, emit0)
                coded = _extract_top([s_s[buf, 1, 8 * k:8 * k + 8, ls] for k in range(key_tiles)], ntop, emit1,
                                     rank_codes=True)
                vals_s[0, ntop:vrows, ls] = neg
                vals_s[1, ntop:vrows, ls] = neg
                v0 = vals_s[0, :, ls]
                v1 = vals_s[1, :, ls]
                cand = ([top0[0] + v1[8 * k:8 * k + 8] for k in range(vrows // 8)]
                        + [top0[a] + v1[0:8] for a in range(1, 8)]
                        + [v0[8 * k:8 * k + 8] + rep_s[0, :, ls] for k in range(1, vrows // 8)])
                got = {}

                def emit(rank, m):
                    got[rank] = m
                _extract_top(cand, ntop, emit)
                z = sum(jnp.exp(got[k] - got[0]) for k in range(PEER_TOPK))
                thr = 0.5 * (got[PEER_TOPK - 1] + got[PEER_TOPK])
                scale = 0.5 / z
                for k in range(0, key_tiles, 2):
                    rows = slice(8 * k, 8 * k + 16)
                    s0 = s_s[buf, 0, rows, ls]
                    slack = jnp.concatenate([thr, thr], axis=0) - s0
                    n = jnp.zeros_like(s0)
                    for b in range(ntop):
                        vb = rep_s[b, :, ls]
                        n = jnp.where(jnp.concatenate([vb, vb], axis=0) >= slack, float(b + 1), n)
                    n_s[k // 2, h, :, ls] = _pair_words(n)
                    m0 = jnp.concatenate([top0[0], top0[0]], axis=0)
                    e0 = jnp.exp(s0 - m0) * jnp.concatenate([scale, scale], axis=0)
                    e0_s[k // 2, h, :, ls] = _pair_words(e0.astype(BF16).astype(F32))
                    m1 = rep_s[0, :, ls]
                    e1 = jnp.exp(s_s[buf, 1, rows, ls] - jnp.concatenate([m1, m1], axis=0))
                    e1_s[h, rows, ls] = e1.astype(BF16)
                    t = jnp.concatenate([coded[k], coded[k + 1]], axis=0)
                    rank = jnp.where(t <= -0.5 * RANK_CODE_BASE * RANK_CODE_SCALE,
                                     t * (-1.0 / RANK_CODE_SCALE) - RANK_CODE_BASE, RANK_NONE)
                    rk_s[h, rows, ls] = rank.astype(BF16)

        def head_pair(m, carry):
            h = 2 * m
            head_scores(h + 1, 1)
            head_tables(h, 0)
            head_scores(jnp.minimum(h + 2, PEER_HEADS - 1), 0)
            head_tables(h + 1, 1)
            return carry

        head_scores(0, 0)
        lax.fori_loop(0, PEER_HEADS // 2, head_pair, 0)
        acc_s[...] = jnp.zeros_like(acc_s)

    def scores(sub):
        st_s[sub % 2] = _dot(u_ref[sub * PEER_SUB:(sub + 1) * PEER_SUB, :], xn_s[...])

    def gates(sub):
        buf = sub % 2
        for h in range(PEER_HEADS):
            for ii in range(per_sub):
                row = sub * per_sub + ii
                rows_s[buf, h, ii, 0] = jnp.broadcast_to(n_s[c, h, row:row + 1, :], (8, tb))
                rows_s[buf, h, ii, 1] = jnp.broadcast_to(e0_s[c, h, row:row + 1, :], (8, tb))
        reps = ROW_TILE // 16
        for ls in lane_tiles:
            for jt in range(N_KEYS // ROW_TILE):
                js = slice(jt * ROW_TILE, (jt + 1) * ROW_TILE)
                gate = [jnp.zeros((ROW_TILE, LANE_TILE), BF16) for _ in range(per_sub)]
                for h in range(PEER_HEADS):
                    rk = rk_s[h, js, ls]
                    e1 = e1_s[h, js, ls]
                    for ii in range(per_sub):
                        n = jnp.tile(pltpu.bitcast(rows_s[buf, h, ii, 0, :, ls], BF16), (reps, 1))
                        e = jnp.tile(pltpu.bitcast(rows_s[buf, h, ii, 1, :, ls], BF16), (reps, 1))
                        gate[ii] = gate[ii] + e * jnp.where(rk < n, e1, jnp.zeros_like(e1))
                for ii in range(per_sub):
                    lo = ii * N_KEYS + jt * ROW_TILE
                    s = st_s[buf, lo:lo + ROW_TILE, ls]
                    act = s + s * jnp.tanh(s * (GELU_K0 + GELU_K1 * s * s))
                    wt_s[buf, lo:lo + ROW_TILE, ls] = act.astype(BF16) * gate[ii]

    scores(0)
    for sub in range(nsub):
        if sub + 1 < nsub:
            scores(sub + 1)
        gates(sub)
        acc_s[...] += _dot(vt_ref[:, sub * PEER_SUB:(sub + 1) * PEER_SUB], wt_s[sub % 2])

    @pl.when(c == pl.num_programs(1) - 1)
    def _finish():
        out = h_ref[...] + acc_s[...].T
        out_ref[...] = _rms(out, og_ref[...]) if norm_output else out


def _peer(h, g, wqt, sk, u, vt, out_gain, norm_output):
    t, d = h.shape
    tb = PEER_TOKENS
    n_exp = u.shape[0]
    vrows = 24
    keys_per_step = PEER_EXPERTS // N_KEYS
    assert keys_per_step == 16
    step_tables = (N_KEYS // keys_per_step, PEER_HEADS, keys_per_step, tb)
    return pl.pallas_call(
        functools.partial(_peer_kernel, norm_output=norm_output),
        grid=(t // tb, n_exp // PEER_EXPERTS),
        in_specs=[pl.BlockSpec((tb, d), lambda i, c: (i, 0), pipeline_mode=pl.Buffered(1)),
                  _full(g.shape),
                  pl.BlockSpec(wqt.shape, lambda i, c: (0, 0), pipeline_mode=pl.Buffered(1)),
                  _full(sk.shape),
                  pl.BlockSpec((PEER_EXPERTS, d), lambda i, c: (c, 0)),
                  pl.BlockSpec((d, PEER_EXPERTS), lambda i, c: (0, c)),
                  _full(out_gain.shape)],
        out_specs=pl.BlockSpec((tb, d), lambda i, c: (i, 0)),
        out_shape=jax.ShapeDtypeStruct((t, d), F32),
        scratch_shapes=[pltpu.VMEM((d, tb), BF16),
                        pltpu.VMEM((2, 2, N_KEYS, tb), F32),
                        pltpu.VMEM((2, vrows, tb), F32),
                        pltpu.VMEM((vrows, 8, tb), F32),
                        pltpu.VMEM((PEER_HEADS, N_KEYS, tb), BF16),
                        pltpu.VMEM((PEER_HEADS, N_KEYS, tb), BF16),
                        pltpu.VMEM(step_tables, jnp.uint32),
                        pltpu.VMEM(step_tables, jnp.uint32),
                        pltpu.VMEM((2, PEER_HEADS, PEER_SUB // N_KEYS, 2, 8, tb), jnp.uint32),
                        pltpu.VMEM((2, PEER_SUB, tb), F32),
                        pltpu.VMEM((2, PEER_SUB, tb), BF16),
                        pltpu.VMEM((d, tb), F32)],
        compiler_params=_params("parallel", "arbitrary"),
        name="peer_ffn",
    )(h, g, wqt, sk, u, vt, out_gain)


def _rot_cols(w):
    half = QK_ROPE // 2
    return jnp.concatenate([-w[:, half:], w[:, :half]], axis=1)


def _pad_rope(w):
    z = jnp.zeros((w.shape[0], QK_NOPE), w.dtype)
    z2 = jnp.zeros((w.shape[0], HEAD_PAD - QK_NOPE - QK_ROPE), w.dtype)
    return jnp.concatenate([z, w, z2], axis=1)


def _mla_weights(w_in, w_q_up, w_kv_up):
    w_kr = w_in[:, Q_LORA + KV_LORA:]
    win = jnp.concatenate([w_in[:, :Q_LORA + KV_LORA], _pad_rope(w_kr), _pad_rope(_rot_cols(w_kr))], axis=1)
    wq = w_q_up.reshape(Q_LORA, MLA_HEADS, QK_NOPE + QK_ROPE)
    pad = jnp.zeros((Q_LORA, MLA_HEADS, HEAD_PAD - QK_NOPE - QK_ROPE), w_q_up.dtype)
    nope0 = jnp.zeros((Q_LORA, MLA_HEADS, QK_NOPE), w_q_up.dtype)
    rope = wq[:, :, QK_NOPE:]
    half = QK_ROPE // 2
    rope_rot = jnp.concatenate([-rope[:, :, half:], rope[:, :, :half]], axis=2)
    q_plain = jnp.concatenate([wq, pad], axis=2).reshape(Q_LORA, MLA_HEADS * HEAD_PAD)
    q_rot = jnp.concatenate([nope0, rope_rot, pad], axis=2).reshape(Q_LORA, MLA_HEADS * HEAD_PAD)
    wkv = w_kv_up.reshape(KV_LORA, MLA_HEADS, QK_NOPE + V_HEAD)
    kpad = jnp.zeros((KV_LORA, MLA_HEADS, HEAD_PAD - QK_NOPE), w_kv_up.dtype)
    k_nope = jnp.concatenate([wkv[:, :, :QK_NOPE], kpad], axis=2).reshape(KV_LORA, MLA_HEADS * HEAD_PAD)
    v = wkv[:, :, QK_NOPE:].reshape(KV_LORA, MLA_HEADS * V_HEAD)
    return (win.astype(BF16), jnp.concatenate([q_plain, q_rot], axis=1).astype(BF16),
            jnp.concatenate([k_nope, v], axis=1).astype(BF16))


def _rope_tables(pos):
    half = QK_ROPE // 2
    freqs = ROPE_THETA ** (-jnp.arange(half, dtype=F32) / half)
    ang = pos.astype(F32)[:, None] * freqs
    n = pos.shape[0]
    ones = jnp.ones((n, QK_NOPE), F32)
    zeros = jnp.zeros((n, QK_NOPE), F32)
    tail = jnp.zeros((n, HEAD_PAD - QK_NOPE - QK_ROPE), F32)
    cos = jnp.concatenate([ones, jnp.cos(ang), jnp.cos(ang), tail], axis=1)
    sin = jnp.concatenate([zeros, jnp.sin(ang), jnp.sin(ang), tail], axis=1)
    return cos, sin


def kernel(x, positions, meta_tokens, mixer_norm, ffn_norm, final_norm, mla_w_in, mla_q_norm, mla_w_q_up, mla_kv_norm, mla_w_kv_up, mla_w_out, lru_w_in, lru_conv_w, lru_conv_b, lru_w_gate_a, lru_b_gate_a, lru_w_gate_x, lru_b_gate_x, lru_lambda, lru_w_out, peer_w_query, peer_sub_keys, peer_u, peer_v):
    nb, seq, d = x.shape
    assert nb % 8 == 0 and (nb * N_META) % ROW_BLOCK == 0 and (nb * seq) % ROW_BLOCK == 0
    assert (nb * (seq + N_META)) % PEER_TOKENS == 0 and seq % min(ATTN_BLOCK, seq) == 0
    row = lambda v: v.reshape(1, -1).astype(F32)

    h = jnp.concatenate([x.transpose(1, 0, 2).reshape(seq * nb, d),
                         jnp.repeat(meta_tokens.astype(x.dtype), nb, axis=0)], axis=0)
    pos = jnp.concatenate([positions.astype(jnp.int32).T.reshape(-1) + N_META,
                           jnp.repeat(jnp.arange(N_META, dtype=jnp.int32), nb)], axis=0)
    cos, sin = _rope_tables(pos)

    def peer(h, layer, norm_output):
        wqt = peer_w_query[layer].T.astype(BF16)
        return _peer(h, row(ffn_norm[layer]), wqt, peer_sub_keys[layer].astype(BF16),
                     peer_u[layer].astype(BF16), peer_v[layer].astype(BF16).T, row(final_norm), norm_output)

    win, wq, wkv = _mla_weights(mla_w_in[0], mla_w_q_up[0], mla_w_kv_up[0])
    q, k, v = _mla_qkv(h, row(mixer_norm[0]), win, row(mla_q_norm[0]), row(mla_kv_norm[0]), wq, wkv, cos, sin)
    o = _attention(q, k, v, nb, seq)
    h = _proj_res(o, mla_w_out[0].astype(BF16), h)
    h = peer(h, 0, False)

    h = _lru_block(h, row(mixer_norm[1]), lru_w_in[0].astype(BF16), lru_conv_w[0].astype(F32),
                   row(lru_conv_b[0]), lru_w_gate_a[0].astype(BF16), row(lru_b_gate_a[0]),
                   lru_w_gate_x[0].astype(BF16), row(lru_b_gate_x[0]), row(lru_lambda[0]),
                   lru_w_out[0].astype(BF16), nb)
    out = peer(h, 1, True)
    return out.reshape(seq + N_META, nb, d).transpose(1, 0, 2)[:, :seq]
```
